```python
import math
import jax, jax.numpy as jnp
from jax import lax
import numpy as np


D_MODEL = 1024
BATCH = 8
SEQ = 4096
DEPTH = 1
DEC_BATCH = 2
DEC_SEQ = 8192
PAST_LEN = 128

HEAD_DIM = 64
D_MIX = D_MODEL
D_A = D_MIX // 2
D_B = D_MIX - D_A
HA = D_A // HEAD_DIM
HKV_A = 2
G_A = HA // HKV_A
HB = D_B // (2 * HEAD_DIM)
WINDOW = 128
BLK = 128
N_BUCKETS = 32
HALF_BUCKETS = N_BUCKETS // 2
MAX_EXACT = HALF_BUCKETS // 2
MAX_DIST = 128
ALPHA = (2.0 * DEPTH) ** 0.25
BETA = (8.0 * DEPTH) ** -0.25
LN_EPS = 1e-5
SUBLN_EPS = 1e-5
NEG_INF = -1e30
SPLIT_SIZES = (D_A, HKV_A * HEAD_DIM, HKV_A * HEAD_DIM, D_A, D_B, D_B, D_B, D_B)
SPLIT_OFFSETS = tuple(int(v) for v in np.cumsum(SPLIT_SIZES)[:-1])
D_IN = int(sum(SPLIT_SIZES))

kernel_name = 'hybrid_swa_diffattn_deepnorm_encoder'


def t5_bucket(rel):
    sign = jnp.where(rel > 0, HALF_BUCKETS, 0)
    n = jnp.abs(rel)
    nf = jnp.maximum(n, 1).astype(jnp.float32)
    large = MAX_EXACT + (jnp.log(nf / MAX_EXACT) / math.log(MAX_DIST / MAX_EXACT)
                         * (HALF_BUCKETS - MAX_EXACT)).astype(jnp.int32)
    large = jnp.minimum(large, HALF_BUCKETS - 1)
    return (sign + jnp.where(n < MAX_EXACT, n, large)).astype(jnp.int32)


def layer_norm(x, g, b):
    xf = x.astype(jnp.float32)
    mu = jnp.mean(xf, -1, keepdims=True)
    var = jnp.mean(jnp.square(xf - mu), -1, keepdims=True)
    y = (xf - mu) * lax.rsqrt(var + LN_EPS) * g.astype(jnp.float32) + b.astype(jnp.float32)
    return y.astype(x.dtype)


def rms_norm(x, w):
    xf = x.astype(jnp.float32)
    y = xf * lax.rsqrt(jnp.mean(jnp.square(xf), -1, keepdims=True) + SUBLN_EPS) * w.astype(jnp.float32)
    return y.astype(x.dtype)


def window_gqa(q, k, v, sink, rel_bias):
    B, S, _ = q.shape
    NB = S // BLK
    qb = q.reshape(B, NB, BLK, HKV_A, G_A, HEAD_DIM)
    pad = ((0, 0), (WINDOW, WINDOW), (0, 0), (0, 0))
    kp = jnp.pad(k.reshape(B, S, HKV_A, HEAD_DIM), pad).reshape(B, NB + 2, BLK, HKV_A, HEAD_DIM)
    vp = jnp.pad(v.reshape(B, S, HKV_A, HEAD_DIM), pad).reshape(B, NB + 2, BLK, HKV_A, HEAD_DIM)
    kw = jnp.concatenate([kp[:, :-2], kp[:, 1:-1], kp[:, 2:]], axis=2)
    vw = jnp.concatenate([vp[:, :-2], vp[:, 1:-1], vp[:, 2:]], axis=2)
    a = jnp.arange(BLK, dtype=jnp.int32)[:, None]
    kk = jnp.arange(3 * BLK, dtype=jnp.int32)[None, :]
    rel = kk - BLK - a
    bias = rel_bias[:, :HA][t5_bucket(rel)].astype(jnp.float32)
    bias = bias.transpose(2, 0, 1).reshape(HKV_A, G_A, BLK, 3 * BLK)
    kpos = jnp.arange(NB, dtype=jnp.int32)[:, None, None] * BLK + kk[None] - BLK
    valid = (jnp.abs(rel) <= WINDOW)[None] & (kpos >= 0) & (kpos < S)
    logits = jnp.einsum('bnqhgd,bnkhd->bnhgqk', qb, kw).astype(jnp.float32) * (HEAD_DIM ** -0.5)
    logits = jnp.where(valid[None, :, None, None], logits + bias[None, None], NEG_INF)
    s = sink.reshape(HKV_A, G_A)[None, None, :, :, None, None].astype(jnp.float32)
    m = jnp.maximum(jnp.max(logits, -1, keepdims=True), s)
    e = jnp.exp(logits - m)
    p = e / (jnp.sum(e, -1, keepdims=True) + jnp.exp(s - m))
    o = jnp.einsum('bnhgqk,bnkhd->bnqhgd', p.astype(v.dtype), vw)
    return o.reshape(B, S, HA * HEAD_DIM)


def diff_attention(q, k, v, lq1, lk1, lq2, lk2, subln_w, rel_bias, lambda_init):
    B, S, _ = q.shape
    NB = S // BLK
    E = 2 * HEAD_DIM
    qh = q.reshape(B, NB, BLK, HB, 2, HEAD_DIM).transpose(1, 0, 2, 3, 4, 5)
    kh = k.reshape(B, S, HB, 2, HEAD_DIM)
    vh = v.reshape(B, S, HB, E)
    lam = (jnp.exp(jnp.sum(lq1.astype(jnp.float32) * lk1.astype(jnp.float32)))
           - jnp.exp(jnp.sum(lq2.astype(jnp.float32) * lk2.astype(jnp.float32))) + lambda_init)
    table = rel_bias[:, HA:].astype(jnp.float32)
    kpos = jnp.arange(S, dtype=jnp.int32)

    def block(args):
        qb, i = args
        qpos = i * BLK + jnp.arange(BLK, dtype=jnp.int32)
        bias = table[t5_bucket(kpos[None, :] - qpos[:, None])].transpose(2, 0, 1)
        logits = jnp.einsum('bqhmd,bkhmd->bhmqk', qb, kh).astype(jnp.float32) * (HEAD_DIM ** -0.5)
        p = jax.nn.softmax(logits + bias[None, :, None], axis=-1)
        w = p[:, :, 0] - lam * p[:, :, 1]
        return jnp.einsum('bhqk,bkhe->bqhe', w.astype(vh.dtype), vh)

    o = lax.map(block, (qh, jnp.arange(NB, dtype=jnp.int32)))
    o = o.transpose(1, 0, 2, 3, 4).reshape(B, S, HB, E)
    o = rms_norm(o, subln_w) * (1.0 - lambda_init)
    return o.reshape(B, S, HB * E)


def hybrid_layer(x, c, w_in, w_out, w_ada, b_ada, ln_g, ln_b, sink,
                 lq1, lk1, lq2, lk2, subln_w, rel_bias, lambda_init):
    mod = jax.nn.silu(c) @ w_ada + b_ada
    shift, scale, gate = jnp.split(mod, 3, axis=-1)
    u = x * (1.0 + scale[:, None]) + shift[:, None]
    proj = u @ w_in
    qa, ka, va, ga, qb, kb, vb, gb = jnp.split(proj, SPLIT_OFFSETS, axis=-1)
    oa = window_gqa(qa, ka, va, sink, rel_bias) * jax.nn.silu(ga)
    ob = diff_attention(qb, kb, vb, lq1, lk1, lq2, lk2, subln_w, rel_bias, lambda_init) * jax.nn.silu(gb)
    h = jnp.concatenate([oa, ob], axis=-1) @ w_out
    return layer_norm(ALPHA * x + gate[:, None] * h, ln_g, ln_b)


def setup_inputs(seed: int = 0) -> dict:
    key = jax.random.key(seed)
    ks = jax.random.split(key, 20)
    f32 = jnp.float32
    col_scale = np.ones((D_IN,), np.float32)
    off = np.concatenate([[0], np.cumsum(SPLIT_SIZES)])
    for idx in (2, 6):
        col_scale[off[idx]:off[idx + 1]] = BETA
    w_in = jax.random.normal(ks[4], (DEPTH, D_MODEL, D_IN), f32) * (D_MODEL ** -0.5) * jnp.asarray(col_scale)
    return {
        'x_prompt': jax.random.normal(ks[0], (BATCH, SEQ, D_MODEL), f32),
        'x_sample': jax.random.normal(ks[1], (DEC_BATCH, DEC_SEQ, D_MODEL), f32),
        'c_prompt': jax.random.normal(ks[2], (BATCH, D_MODEL), f32),
        'c_sample': jax.random.normal(ks[3], (DEC_BATCH, D_MODEL), f32),
        'w_in': w_in,
        'w_out': jax.random.normal(ks[5], (DEPTH, D_MIX, D_MODEL), f32) * (D_MIX ** -0.5) * BETA,
        'w_ada': jax.random.normal(ks[6], (DEPTH, D_MODEL, 3 * D_MODEL), f32) * (D_MODEL ** -0.5),
        'b_ada': jax.random.normal(ks[7], (DEPTH, 3 * D_MODEL), f32) * 0.02,
        'ln_g': 1.0 + 0.02 * jax.random.normal(ks[8], (DEPTH, D_MODEL), f32),
        'ln_b': 0.02 * jax.random.normal(ks[9], (DEPTH, D_MODEL), f32),
        'attn_sink': jax.random.normal(ks[10], (DEPTH, HA), f32),
        'lambda_q1': 0.1 * jax.random.normal(ks[11], (DEPTH, HEAD_DIM), f32),
        'lambda_k1': 0.1 * jax.random.normal(ks[12], (DEPTH, HEAD_DIM), f32),
        'lambda_q2': 0.1 * jax.random.normal(ks[13], (DEPTH, HEAD_DIM), f32),
        'lambda_k2': 0.1 * jax.random.normal(ks[14], (DEPTH, HEAD_DIM), f32),
        'subln_w': 1.0 + 0.02 * jax.random.normal(ks[15], (DEPTH, 2 * HEAD_DIM), f32),
        'rel_bias': 0.5 * jax.random.normal(ks[16], (N_BUCKETS, HA + HB), f32),
    }


def reference(x_prompt, x_sample, c_prompt, c_sample, w_in, w_out, w_ada, b_ada,
              ln_g, ln_b, attn_sink, lambda_q1, lambda_k1, lambda_q2, lambda_k2,
              subln_w, rel_bias):
    xp = x_prompt
    xs = x_sample
    for l in range(DEPTH):
        lambda_init = 0.8 - 0.6 * math.exp(-0.3 * l)
        xp = hybrid_layer(xp, c_prompt, w_in[l], w_out[l], w_ada[l], b_ada[l], ln_g[l], ln_b[l],
                          attn_sink[l], lambda_q1[l], lambda_k1[l], lambda_q2[l], lambda_k2[l],
                          subln_w[l], rel_bias, lambda_init)
        xs = hybrid_layer(xs, c_sample, w_in[l], w_out[l], w_ada[l], b_ada[l], ln_g[l], ln_b[l],
                          attn_sink[l], lambda_q1[l], lambda_k1[l], lambda_q2[l], lambda_k2[l],
                          subln_w[l], rel_bias, lambda_init)
    y_prompt = xp
    y_sample = xs
    return (y_prompt, y_sample)
```

```python
import functools
import math

import jax
import jax.numpy as jnp
from jax import lax
from jax.experimental import pallas as pl
from jax.experimental.pallas import tpu as pltpu

F32 = jnp.float32
BF16 = jnp.bfloat16

D_MODEL = 1024
HEAD_DIM = 64
HA = 8
HKV_A = 2
G_A = HA // HKV_A
HB = 4
E_B = 2 * HEAD_DIM
D_A = HA * HEAD_DIM
D_KV_A = HKV_A * HEAD_DIM
D_B = HB * E_B
WINDOW = 128
BLK = 128
N_BUCKETS = 32
HALF_BUCKETS = N_BUCKETS // 2
MAX_EXACT = HALF_BUCKETS // 2
MAX_DIST = 128
DEPTH = 1
ALPHA = (2.0 * DEPTH) ** 0.25
LN_EPS = 1e-5
SUBLN_EPS = 1e-5
NEG_INF = -1e30
LOG2E = 1.4426950408889634
Q_SCALE = HEAD_DIM ** -0.5 * LOG2E

_OFF = {}
_o = 0
for _name, _w in (("qa", D_A), ("ka", D_KV_A), ("va", D_KV_A), ("ga", D_A),
                  ("qb", D_B), ("kb", D_B), ("vb", D_B), ("gb", D_B)):
    _OFF[_name] = (_o, _o + _w)
    _o += _w

ROW_TILE = 512
DIFF_TQ = 256
DIFF_TK = 512
VMEM_LIMIT = 56 * 1024 * 1024


def _t5_bucket(rel):
    sign = jnp.where(rel > 0, HALF_BUCKETS, 0)
    n = jnp.abs(rel)
    nf = jnp.maximum(n, 1).astype(jnp.float32)
    large = MAX_EXACT + (jnp.log(nf / MAX_EXACT) / math.log(MAX_DIST / MAX_EXACT)
                         * (HALF_BUCKETS - MAX_EXACT)).astype(jnp.int32)
    large = jnp.minimum(large, HALF_BUCKETS - 1)
    return (sign + jnp.where(n < MAX_EXACT, n, large)).astype(jnp.int32)


def _silu(g):
    return g * jax.nn.sigmoid(g)


def _mod_kernel(c_ref, w_ref, b_ref, o_ref):
    a = _silu(c_ref[...])
    a_hi = a.astype(BF16)
    a_lo = (a - a_hi.astype(F32)).astype(BF16)
    w = w_ref[...]
    w_hi = w.astype(BF16)
    w_lo = (w - w_hi.astype(F32)).astype(BF16)
    acc = jnp.dot(a_hi, w_hi, preferred_element_type=F32)
    acc += jnp.dot(a_lo, w_hi, preferred_element_type=F32)
    acc += jnp.dot(a_hi, w_lo, preferred_element_type=F32)
    o_ref[...] = acc + b_ref[...]


def _modulation(c_all, w_ada, b_ada):
    rows = c_all.shape[0]
    n = w_ada.shape[1]
    tn = 512
    return pl.pallas_call(
        _mod_kernel,
        grid=(n // tn,),
        in_specs=[pl.BlockSpec((rows, D_MODEL), lambda j: (0, 0)),
                  pl.BlockSpec((D_MODEL, tn), lambda j: (0, j)),
                  pl.BlockSpec((1, tn), lambda j: (0, j))],
        out_specs=pl.BlockSpec((rows, tn), lambda j: (0, j)),
        out_shape=jax.ShapeDtypeStruct((rows, n), F32),
        name="modulation",
    )(c_all, w_ada, b_ada.reshape(1, n))


def _inproj_kernel(x_ref, shift_ref, scale_ref, wstd_ref, wfm_ref,
                   ka_ref, ga_ref, kb_ref, gb_ref, qat_ref, vat_ref, qbt_ref, vbt_ref):
    u = (x_ref[...] * (1.0 + scale_ref[...]) + shift_ref[...]).astype(BF16)
    r = jnp.dot(u, wstd_ref[...], preferred_element_type=F32)
    ka_ref[...] = r[:, 0:128].astype(BF16)
    ga_ref[...] = r[:, 128:640]
    kb_ref[...] = r[:, 640:1152].astype(BF16)
    gb_ref[...] = r[:, 1152:1664]
    rt = lax.dot_general(wfm_ref[...], u, (((1,), (1,)), ((), ())),
                         preferred_element_type=F32)
    qat_ref[...] = (rt[0:512] * Q_SCALE).astype(BF16)
    vat_ref[...] = rt[512:640].astype(BF16)
    qbt_ref[...] = (rt[640:1152] * Q_SCALE).astype(BF16)
    vbt_ref[...] = rt[1152:1664].astype(BF16)


def _in_projection(x, mod4, b_off, w_std, w_fm):
    bsz, seq, _ = x.shape
    tm = ROW_TILE
    n_std = w_std.shape[1]
    n_fm = w_fm.shape[0]
    row = lambda w: pl.BlockSpec((None, tm, w), lambda b, t: (b, t, 0))
    col = lambda h: pl.BlockSpec((None, h, tm), lambda b, t: (b, 0, t))
    vec = lambda k: pl.BlockSpec((None, None, 1, D_MODEL), lambda b, t: (b + b_off, k, 0, 0))
    return pl.pallas_call(
        _inproj_kernel,
        grid=(bsz, seq // tm),
        in_specs=[row(D_MODEL), vec(0), vec(1),
                  pl.BlockSpec((D_MODEL, n_std), lambda b, t: (0, 0)),
                  pl.BlockSpec((n_fm, D_MODEL), lambda b, t: (0, 0))],
        out_specs=[row(D_KV_A), row(D_A), row(D_B), row(D_B),
                   col(D_A), col(D_KV_A), col(D_B), col(D_B)],
        out_shape=[jax.ShapeDtypeStruct((bsz, seq, D_KV_A), BF16),
                   jax.ShapeDtypeStruct((bsz, seq, D_A), F32),
                   jax.ShapeDtypeStruct((bsz, seq, D_B), BF16),
                   jax.ShapeDtypeStruct((bsz, seq, D_B), F32),
                   jax.ShapeDtypeStruct((bsz, D_A, seq), BF16),
                   jax.ShapeDtypeStruct((bsz, D_KV_A, seq), BF16),
                   jax.ShapeDtypeStruct((bsz, D_B, seq), BF16),
                   jax.ShapeDtypeStruct((bsz, D_B, seq), BF16)],
        compiler_params=pltpu.CompilerParams(vmem_limit_bytes=VMEM_LIMIT),
        name="in_projection",
    )(x, mod4, mod4, w_std, w_fm)


def _win_kernel(sink_ref, qt_ref, kl_ref, kc_ref, kr_ref, vl_ref, vc_ref, vr_ref,
                bias_ref, g_ref, o_ref, *, nb):
    i = pl.program_id(1)
    kwin = jnp.concatenate([kl_ref[...], kc_ref[...], kr_ref[...]], axis=0)
    vwin = jnp.concatenate([vl_ref[...], vc_ref[...], vr_ref[...]], axis=1)
    krow = lax.broadcasted_iota(jnp.int32, (3 * BLK, BLK), 0)
    off_l = jnp.where(i == 0, NEG_INF, 0.0).astype(F32)
    off_r = jnp.where(i == nb - 1, NEG_INF, 0.0).astype(F32)
    off = jnp.where(krow < BLK, off_l, 0.0) + jnp.where(krow >= 2 * BLK, off_r, 0.0)
    zeros = jnp.zeros((HEAD_DIM, BLK), BF16)
    outs = []
    for hk in range(HKV_A):
        blocks, biases, sinks = [], [], []
        for g in range(G_A):
            h = hk * G_A + g
            q = qt_ref[h * HEAD_DIM:(h + 1) * HEAD_DIM, :]
            blocks.append(jnp.concatenate([q, zeros] if hk == 0 else [zeros, q], axis=0))
            biases.append(bias_ref[h] + off)
            sinks.append(jnp.full((1, BLK), sink_ref[h], F32))
        q_aug = jnp.concatenate(blocks, axis=1)
        s = jnp.dot(kwin, q_aug, preferred_element_type=F32)
        s = s + jnp.concatenate(biases, axis=1)
        sink = jnp.concatenate(sinks, axis=1)
        m = jnp.maximum(jnp.max(s, axis=0, keepdims=True), sink)
        p = jnp.exp2(s - m)
        den = jnp.sum(p, axis=0, keepdims=True) + jnp.exp2(sink - m)
        ot = jnp.dot(vwin[hk * HEAD_DIM:(hk + 1) * HEAD_DIM, :], p.astype(BF16),
                     preferred_element_type=F32)
        ot = ot * (1.0 / den)
        for g in range(G_A):
            outs.append(ot[:, g * BLK:(g + 1) * BLK])
    o = jnp.concatenate(outs, axis=0).T
    o_ref[...] = (o * _silu(g_ref[...])).astype(BF16)


def _window_attention(qat, ka, vat, ga, bias_t, sink2):
    bsz, seq, _ = ka.shape
    nb = seq // BLK
    kblk = lambda f: pl.BlockSpec((None, BLK, D_KV_A), lambda b, i: (b, f(i), 0))
    vblk = lambda f: pl.BlockSpec((None, D_KV_A, BLK), lambda b, i: (b, 0, f(i)))
    left = lambda i: jnp.maximum(i - 1, 0)
    mid = lambda i: i
    right = lambda i: jnp.minimum(i + 1, nb - 1)
    return pl.pallas_call(
        functools.partial(_win_kernel, nb=nb),
        grid=(bsz, nb),
        in_specs=[pl.BlockSpec(memory_space=pltpu.SMEM),
                  pl.BlockSpec((None, D_A, BLK), lambda b, i: (b, 0, i)),
                  kblk(left), kblk(mid), kblk(right),
                  vblk(left), vblk(mid), vblk(right),
                  pl.BlockSpec((HA, 3 * BLK, BLK), lambda b, i: (0, 0, 0)),
                  pl.BlockSpec((None, BLK, D_A), lambda b, i: (b, i, 0))],
        out_specs=pl.BlockSpec((None, BLK, D_A), lambda b, i: (b, i, 0)),
        out_shape=jax.ShapeDtypeStruct((bsz, seq, D_A), BF16),
        compiler_params=pltpu.CompilerParams(vmem_limit_bytes=VMEM_LIMIT),
        name="window_attention",
    )(sink2, qat, ka, ka, ka, vat, vat, vat, bias_t, ga)


def _diff_kernel(qt_ref, k_ref, vt_ref, strip_ref, g_ref, w_ref, lq1_ref, lk1_ref, lq2_ref, lk2_ref,
                 o_ref, m_sc, l_sc, acc_sc, *, tq, tk, nk, lambda_init):
    i = pl.program_id(2)
    qt = qt_ref[...]
    qrow = lax.broadcasted_iota(jnp.int32, (E_B, tq), 0)
    zero = jnp.zeros_like(qt)
    q_aug = jnp.concatenate([jnp.where(qrow < HEAD_DIM, qt, zero),
                             jnp.where(qrow >= HEAD_DIM, qt, zero)], axis=1)
    m_sc[...] = jnp.full(m_sc.shape, NEG_INF, F32)
    l_sc[...] = jnp.zeros(l_sc.shape, F32)
    acc_sc[...] = jnp.zeros(acc_sc.shape, F32)

    def body(j, carry):
        k0 = pl.multiple_of(j * tk, tk)
        s = jnp.dot(k_ref[pl.ds(k0, tk), :], q_aug, preferred_element_type=F32)
        r0 = jnp.clip(j * tk - i * tq, -2 * tk, tk) + 2 * tk
        bias = strip_ref[pl.ds(pl.multiple_of(r0, 128), tk), :]
        s = s + jnp.concatenate([bias, bias], axis=1)
        m_prev = m_sc[...]
        m_new = jnp.maximum(m_prev, jnp.max(s, axis=0, keepdims=True))
        alpha = jnp.exp2(m_prev - m_new)
        p = jnp.exp2(s - m_new)
        l_sc[...] = alpha * l_sc[...] + jnp.sum(p, axis=0, keepdims=True)
        pv = jnp.dot(vt_ref[:, pl.ds(k0, tk)], p.astype(BF16), preferred_element_type=F32)
        acc_sc[...] = acc_sc[...] * alpha + pv
        m_sc[...] = m_new
        return carry

    lax.fori_loop(0, nk, body, 0)

    lam = (jnp.exp(jnp.sum(lq1_ref[...] * lk1_ref[...], axis=1, keepdims=True))
           - jnp.exp(jnp.sum(lq2_ref[...] * lk2_ref[...], axis=1, keepdims=True)) + lambda_init)
    o_all = acc_sc[...] * (1.0 / l_sc[...])
    o = o_all[:, :tq] - lam * o_all[:, tq:]
    ms = jnp.mean(o * o, axis=0, keepdims=True)
    y = o * lax.rsqrt(ms + SUBLN_EPS) * w_ref[...] * (1.0 - lambda_init)
    o_ref[...] = (y.T * _silu(g_ref[...])).astype(BF16)


def _diff_attention(qbt, kb, vbt, gb, strip, subln_w2, lq1, lk1, lq2, lk2, lambda_init):
    bsz, seq, _ = kb.shape
    tq, tk = DIFF_TQ, DIFF_TK
    nk = seq // tk
    lam_spec = pl.BlockSpec((1, HEAD_DIM), lambda b, h, i: (0, 0))
    return pl.pallas_call(
        functools.partial(_diff_kernel, tq=tq, tk=tk, nk=nk, lambda_init=lambda_init),
        grid=(bsz, HB, seq // tq),
        in_specs=[pl.BlockSpec((None, E_B, tq), lambda b, h, i: (b, h, i)),
                  pl.BlockSpec((None, seq, E_B), lambda b, h, i: (b, 0, h)),
                  pl.BlockSpec((None, E_B, seq), lambda b, h, i: (b, h, 0)),
                  pl.BlockSpec((None, 4 * tk, tq), lambda b, h, i: (h, 0, 0)),
                  pl.BlockSpec((None, tq, E_B), lambda b, h, i: (b, i, h)),
                  pl.BlockSpec((E_B, 1), lambda b, h, i: (0, 0)),
                  lam_spec, lam_spec, lam_spec, lam_spec],
        out_specs=pl.BlockSpec((None, tq, E_B), lambda b, h, i: (b, i, h)),
        out_shape=jax.ShapeDtypeStruct((bsz, seq, D_B), BF16),
        scratch_shapes=[pltpu.VMEM((1, 2 * tq), F32),
                        pltpu.VMEM((1, 2 * tq), F32),
                        pltpu.VMEM((E_B, 2 * tq), F32)],
        compiler_params=pltpu.CompilerParams(vmem_limit_bytes=VMEM_LIMIT),
        name="diff_attention",
    )(qbt, kb, vbt, strip, gb, subln_w2, lq1, lk1, lq2, lk2)


def _outproj_kernel(oa_ref, ob_ref, x_ref, gate_ref, wa_ref, wb_ref, lng_ref, lnb_ref, y_ref):
    h = jnp.dot(oa_ref[...], wa_ref[...], preferred_element_type=F32)
    h += jnp.dot(ob_ref[...], wb_ref[...], preferred_element_type=F32)
    z = ALPHA * x_ref[...] + gate_ref[...] * h
    mu = jnp.mean(z, axis=-1, keepdims=True)
    zc = z - mu
    var = jnp.mean(zc * zc, axis=-1, keepdims=True)
    y_ref[...] = zc * lax.rsqrt(var + LN_EPS) * lng_ref[...] + lnb_ref[...]


def _out_projection(oa, ob, x, mod4, b_off, w_out_a, w_out_b, ln_g2, ln_b2):
    bsz, seq, _ = x.shape
    tm = ROW_TILE
    row = lambda w: pl.BlockSpec((None, tm, w), lambda b, t: (b, t, 0))
    full = lambda r, c: pl.BlockSpec((r, c), lambda b, t: (0, 0))
    return pl.pallas_call(
        _outproj_kernel,
        grid=(bsz, seq // tm),
        in_specs=[row(D_A), row(D_B), row(D_MODEL),
                  pl.BlockSpec((None, None, 1, D_MODEL), lambda b, t: (b + b_off, 2, 0, 0)),
                  full(D_A, D_MODEL), full(D_B, D_MODEL), full(1, D_MODEL), full(1, D_MODEL)],
        out_specs=row(D_MODEL),
        out_shape=jax.ShapeDtypeStruct((bsz, seq, D_MODEL), F32),
        compiler_params=pltpu.CompilerParams(vmem_limit_bytes=VMEM_LIMIT),
        name="out_projection",
    )(oa, ob, x, mod4, w_out_a, w_out_b, ln_g2, ln_b2)


def _bias_tables(rel_bias):
    kk = jnp.arange(3 * BLK, dtype=jnp.int32)[:, None]
    a = jnp.arange(BLK, dtype=jnp.int32)[None, :]
    rel = kk - BLK - a
    win = rel_bias[:, :HA][_t5_bucket(rel)].astype(F32) * LOG2E
    win = jnp.where((jnp.abs(rel) <= WINDOW)[:, :, None], win, NEG_INF).transpose(2, 0, 1)
    r = jnp.arange(-2 * DIFF_TK, 2 * DIFF_TK, dtype=jnp.int32)[:, None]
    a = jnp.arange(DIFF_TQ, dtype=jnp.int32)[None, :]
    strip = (rel_bias[:, HA:][_t5_bucket(r - a)].astype(F32) * LOG2E).transpose(2, 0, 1)
    return win, strip


def _layer(x, mod4, b_off, w_std, w_fm, w_out_a, w_out_b, ln_g2, ln_b2, sink2, win_bias, strip,
           subln_w2, lq1, lk1, lq2, lk2, lambda_init):
    ka, ga, kb, gb, qat, vat, qbt, vbt = _in_projection(x, mod4, b_off, w_std, w_fm)
    oa = _window_attention(qat, ka, vat, ga, win_bias, sink2)
    ob = _diff_attention(qbt, kb, vbt, gb, strip, subln_w2, lq1, lk1, lq2, lk2, lambda_init)
    return _out_projection(oa, ob, x, mod4, b_off, w_out_a, w_out_b, ln_g2, ln_b2)


def kernel(x_prompt, x_sample, c_prompt, c_sample, w_in, w_out, w_ada, b_ada, ln_g, ln_b, attn_sink,
           lambda_q1, lambda_k1, lambda_q2, lambda_k2, subln_w, rel_bias):
    assert w_in.shape[0] == DEPTH
    nbp, nbs = c_prompt.shape[0], c_sample.shape[0]
    pad = (-(nbp + nbs)) % 16
    c_all = jnp.concatenate([c_prompt, c_sample, jnp.zeros((pad, D_MODEL), F32)], axis=0)
    win_bias, strip = _bias_tables(rel_bias)
    xp, xs = x_prompt, x_sample
    for l in range(DEPTH):
        lambda_init = 0.8 - 0.6 * math.exp(-0.3 * l)
        w = w_in[l]
        sl = lambda n: w[:, _OFF[n][0]:_OFF[n][1]]
        w_std = jnp.concatenate([sl("ka"), sl("ga"), sl("kb"), sl("gb")], axis=1).astype(BF16)
        w_fm = jnp.concatenate([sl("qa"), sl("va"), sl("qb"), sl("vb")], axis=1).T.astype(BF16)
        w_out_a = w_out[l, :D_A].astype(BF16)
        w_out_b = w_out[l, D_A:].astype(BF16)
        mod = _modulation(c_all, w_ada[l], b_ada[l])
        mod4 = mod.reshape(mod.shape[0], 3, 1, D_MODEL)
        args = (w_std, w_fm, w_out_a, w_out_b, ln_g[l].reshape(1, -1), ln_b[l].reshape(1, -1),
                attn_sink[l] * LOG2E, win_bias, strip, subln_w[l].reshape(-1, 1),
                lambda_q1[l].reshape(1, -1), lambda_k1[l].reshape(1, -1),
                lambda_q2[l].reshape(1, -1), lambda_k2[l].reshape(1, -1), lambda_init)
        xp = _layer(xp, mod4, 0, *args)
        xs = _layer(xs, mod4, nbp, *args)
    return (xp, xs)
```

```python
import functools
import math

import jax
import jax.numpy as jnp
from jax import lax
from jax.experimental import pallas as pl
from jax.experimental.pallas import tpu as pltpu

F32 = jnp.float32
BF16 = jnp.bfloat16

D_MODEL = 1024
HEAD_DIM = 64
HA = 8
HKV_A = 2
G_A = HA // HKV_A
HB = 4
E_B = 2 * HEAD_DIM
D_A = HA * HEAD_DIM
D_KV_A = HKV_A * HEAD_DIM
D_B = HB * E_B
WINDOW = 128
BLK = 128
N_BUCKETS = 32
HALF_BUCKETS = N_BUCKETS // 2
MAX_EXACT = HALF_BUCKETS // 2
MAX_DIST = 128
DEPTH = 1
ALPHA = (2.0 * DEPTH) ** 0.25
LN_EPS = 1e-5
SUBLN_EPS = 1e-5
NEG_INF = -1e30
LOG2E = 1.4426950408889634
Q_SCALE = HEAD_DIM ** -0.5 * LOG2E

_OFF = {}
_o = 0
for _name, _w in (("qa", D_A), ("ka", D_KV_A), ("va", D_KV_A), ("ga", D_A),
                  ("qb", D_B), ("kb", D_B), ("vb", D_B), ("gb", D_B)):
    _OFF[_name] = (_o, _o + _w)
    _o += _w

ROW_TILE = 512
DIFF_TQ = 256
DIFF_TK = 512
VMEM_LIMIT = 56 * 1024 * 1024


def _t5_bucket(rel):
    sign = jnp.where(rel > 0, HALF_BUCKETS, 0)
    n = jnp.abs(rel)
    nf = jnp.maximum(n, 1).astype(jnp.float32)
    large = MAX_EXACT + (jnp.log(nf / MAX_EXACT) / math.log(MAX_DIST / MAX_EXACT)
                         * (HALF_BUCKETS - MAX_EXACT)).astype(jnp.int32)
    large = jnp.minimum(large, HALF_BUCKETS - 1)
    return (sign + jnp.where(n < MAX_EXACT, n, large)).astype(jnp.int32)


def _silu(g):
    return g * jax.nn.sigmoid(g)


def _mod_kernel(c_ref, w_ref, b_ref, o_ref):
    a = _silu(c_ref[...])
    a_hi = a.astype(BF16)
    a_lo = (a - a_hi.astype(F32)).astype(BF16)
    w = w_ref[...]
    w_hi = w.astype(BF16)
    w_lo = (w - w_hi.astype(F32)).astype(BF16)
    acc = jnp.dot(a_hi, w_hi, preferred_element_type=F32)
    acc += jnp.dot(a_lo, w_hi, preferred_element_type=F32)
    acc += jnp.dot(a_hi, w_lo, preferred_element_type=F32)
    o_ref[...] = acc + b_ref[...]


def _modulation(c_all, w_ada, b_ada):
    rows = c_all.shape[0]
    n = w_ada.shape[1]
    tn = 512
    return pl.pallas_call(
        _mod_kernel,
        grid=(n // tn,),
        in_specs=[pl.BlockSpec((rows, D_MODEL), lambda j: (0, 0)),
                  pl.BlockSpec((D_MODEL, tn), lambda j: (0, j)),
                  pl.BlockSpec((1, tn), lambda j: (0, j))],
        out_specs=pl.BlockSpec((rows, tn), lambda j: (0, j)),
        out_shape=jax.ShapeDtypeStruct((rows, n), F32),
        name="modulation",
    )(c_all, w_ada, b_ada.reshape(1, n))


def _inproj_kernel(x_ref, shift_ref, scale_ref, wstd_ref, wfm_ref,
                   ka_ref, ga_ref, kb_ref, gb_ref, qat_ref, vat_ref, qbt_ref, vbt_ref):
    u = (x_ref[...] * (1.0 + scale_ref[...]) + shift_ref[...]).astype(BF16)
    r = jnp.dot(u, wstd_ref[...], preferred_element_type=F32)
    ka_ref[...] = r[:, 0:128].astype(BF16)
    ga_ref[...] = r[:, 128:640]
    kb_ref[...] = r[:, 640:1152].astype(BF16)
    gb_ref[...] = r[:, 1152:1664]
    rt = lax.dot_general(wfm_ref[...], u, (((1,), (1,)), ((), ())),
                         preferred_element_type=F32)
    qat_ref[...] = (rt[0:512] * Q_SCALE).astype(BF16)
    vat_ref[...] = rt[512:640].astype(BF16)
    qbt_ref[...] = (rt[640:1152] * Q_SCALE).astype(BF16)
    vbt_ref[...] = rt[1152:1664].astype(BF16)


def _in_projection(x, mod4, b_off, w_std, w_fm):
    bsz, seq, _ = x.shape
    tm = ROW_TILE
    n_std = w_std.shape[1]
    n_fm = w_fm.shape[0]
    row = lambda w: pl.BlockSpec((None, tm, w), lambda b, t: (b, t, 0))
    col = lambda h: pl.BlockSpec((None, h, tm), lambda b, t: (b, 0, t))
    vec = lambda k: pl.BlockSpec((None, None, 1, D_MODEL), lambda b, t: (b + b_off, k, 0, 0))
    return pl.pallas_call(
        _inproj_kernel,
        grid=(bsz, seq // tm),
        in_specs=[row(D_MODEL), vec(0), vec(1),
                  pl.BlockSpec((D_MODEL, n_std), lambda b, t: (0, 0)),
                  pl.BlockSpec((n_fm, D_MODEL), lambda b, t: (0, 0))],
        out_specs=[row(D_KV_A), row(D_A), row(D_B), row(D_B),
                   col(D_A), col(D_KV_A), col(D_B), col(D_B)],
        out_shape=[jax.ShapeDtypeStruct((bsz, seq, D_KV_A), BF16),
                   jax.ShapeDtypeStruct((bsz, seq, D_A), F32),
                   jax.ShapeDtypeStruct((bsz, seq, D_B), BF16),
                   jax.ShapeDtypeStruct((bsz, seq, D_B), F32),
                   jax.ShapeDtypeStruct((bsz, D_A, seq), BF16),
                   jax.ShapeDtypeStruct((bsz, D_KV_A, seq), BF16),
                   jax.ShapeDtypeStruct((bsz, D_B, seq), BF16),
                   jax.ShapeDtypeStruct((bsz, D_B, seq), BF16)],
        compiler_params=pltpu.CompilerParams(vmem_limit_bytes=VMEM_LIMIT),
        name="in_projection",
    )(x, mod4, mod4, w_std, w_fm)


def _win_kernel(sink_ref, qt_ref, kl_ref, kc_ref, kr_ref, vl_ref, vc_ref, vr_ref,
                bias_ref, g_ref, o_ref, *, nb):
    i = pl.program_id(1)
    kwin = jnp.concatenate([kl_ref[...], kc_ref[...], kr_ref[...]], axis=0)
    vwin = jnp.concatenate([vl_ref[...], vc_ref[...], vr_ref[...]], axis=1)
    krow = lax.broadcasted_iota(jnp.int32, (3 * BLK, BLK), 0)
    off_l = jnp.where(i == 0, NEG_INF, 0.0).astype(F32)
    off_r = jnp.where(i == nb - 1, NEG_INF, 0.0).astype(F32)
    off = jnp.where(krow < BLK, off_l, 0.0) + jnp.where(krow >= 2 * BLK, off_r, 0.0)
    zeros = jnp.zeros((HEAD_DIM, BLK), BF16)
    outs = []
    for hk in range(HKV_A):
        blocks, biases, sinks = [], [], []
        for g in range(G_A):
            h = hk * G_A + g
            q = qt_ref[h * HEAD_DIM:(h + 1) * HEAD_DIM, :]
            blocks.append(jnp.concatenate([q, zeros] if hk == 0 else [zeros, q], axis=0))
            biases.append(bias_ref[h] + off)
            sinks.append(jnp.full((1, BLK), sink_ref[h], F32))
        q_aug = jnp.concatenate(blocks, axis=1)
        s = jnp.dot(kwin, q_aug, preferred_element_type=F32)
        s = s + jnp.concatenate(biases, axis=1)
        sink = jnp.concatenate(sinks, axis=1)
        m = jnp.maximum(jnp.max(s, axis=0, keepdims=True), sink)
        p = jnp.exp2(s - m)
        den = jnp.sum(p, axis=0, keepdims=True) + jnp.exp2(sink - m)
        ot = jnp.dot(vwin[hk * HEAD_DIM:(hk + 1) * HEAD_DIM, :], p.astype(BF16),
                     preferred_element_type=F32)
        ot = ot * (1.0 / den)
        for g in range(G_A):
            outs.append(ot[:, g * BLK:(g + 1) * BLK])
    o = jnp.concatenate(outs, axis=0).T
    o_ref[...] = (o * _silu(g_ref[...])).astype(BF16)


def _window_attention(qat, ka, vat, ga, bias_t, sink2):
    bsz, seq, _ = ka.shape
    nb = seq // BLK
    kblk = lambda f: pl.BlockSpec((None, BLK, D_KV_A), lambda b, i: (b, f(i), 0))
    vblk = lambda f: pl.BlockSpec((None, D_KV_A, BLK), lambda b, i: (b, 0, f(i)))
    left = lambda i: jnp.maximum(i - 1, 0)
    mid = lambda i: i
    right = lambda i: jnp.minimum(i + 1, nb - 1)
    return pl.pallas_call(
        functools.partial(_win_kernel, nb=nb),
        grid=(bsz, nb),
        in_specs=[pl.BlockSpec(memory_space=pltpu.SMEM),
                  pl.BlockSpec((None, D_A, BLK), lambda b, i: (b, 0, i)),
                  kblk(left), kblk(mid), kblk(right),
                  vblk(left), vblk(mid), vblk(right),
                  pl.BlockSpec((HA, 3 * BLK, BLK), lambda b, i: (0, 0, 0)),
                  pl.BlockSpec((None, BLK, D_A), lambda b, i: (b, i, 0))],
        out_specs=pl.BlockSpec((None, BLK, D_A), lambda b, i: (b, i, 0)),
        out_shape=jax.ShapeDtypeStruct((bsz, seq, D_A), BF16),
        compiler_params=pltpu.CompilerParams(vmem_limit_bytes=VMEM_LIMIT),
        name="window_attention",
    )(sink2, qat, ka, ka, ka, vat, vat, vat, bias_t, ga)


def _diff_kernel(qt_ref, k_ref, vt_ref, strip_ref, g_ref, w_ref, lq1_ref, lk1_ref, lq2_ref, lk2_ref,
                 o_ref, s0_sc, s1_sc, acc_sc, *, tq, tk, nk, lambda_init):
    i = pl.program_id(2)
    qt = qt_ref[...]
    qrow = lax.broadcasted_iota(jnp.int32, (E_B, tq), 0)
    zero = jnp.zeros_like(qt)
    q_aug = jnp.concatenate([jnp.where(qrow < HEAD_DIM, qt, zero),
                             jnp.where(qrow >= HEAD_DIM, qt, zero)], axis=1)
    acc_sc[...] = jnp.zeros(acc_sc.shape, F32)

    def logits(j, s_sc):
        k0 = pl.multiple_of(j * tk, tk)
        s = jnp.dot(k_ref[pl.ds(k0, tk), :], q_aug, preferred_element_type=F32)
        r0 = jnp.clip(j * tk - i * tq, -2 * tk, tk) + 2 * tk
        bias = strip_ref[pl.ds(pl.multiple_of(r0, 128), tk), :]
        s = s + jnp.concatenate([bias, bias], axis=1)
        s_sc[...] = s
        return jnp.max(s, axis=0, keepdims=True)

    def accumulate(j, s_sc, bmax, m_prev, l_prev):
        k0 = pl.multiple_of(j * tk, tk)
        m_new = jnp.maximum(m_prev, bmax)
        alpha = jnp.exp2(m_prev - m_new)
        p = jnp.exp2(s_sc[...] - m_new)
        l_new = alpha * l_prev + jnp.sum(p, axis=0, keepdims=True)
        pv = jnp.dot(vt_ref[:, pl.ds(k0, tk)], p.astype(BF16), preferred_element_type=F32)
        acc_sc[...] = acc_sc[...] * alpha + pv
        return m_new, l_new

    m0 = jnp.full((1, 2 * tq), NEG_INF, F32)
    l0 = jnp.zeros((1, 2 * tq), F32)
    bmax0 = logits(0, s0_sc)

    def pair(t, carry):
        bmax, m, l = carry
        j = 2 * t
        bmax1 = logits(j + 1, s1_sc)
        m, l = accumulate(j, s0_sc, bmax, m, l)
        bmax2 = logits(j + 2, s0_sc)
        m, l = accumulate(j + 1, s1_sc, bmax1, m, l)
        return bmax2, m, l

    bmax, m, l = lax.fori_loop(0, (nk - 2) // 2, pair, (bmax0, m0, l0))
    bmax1 = logits(nk - 1, s1_sc)
    m, l = accumulate(nk - 2, s0_sc, bmax, m, l)
    m, l = accumulate(nk - 1, s1_sc, bmax1, m, l)

    lam = (jnp.exp(jnp.sum(lq1_ref[...] * lk1_ref[...], axis=1, keepdims=True))
           - jnp.exp(jnp.sum(lq2_ref[...] * lk2_ref[...], axis=1, keepdims=True)) + lambda_init)
    o_all = acc_sc[...] * (1.0 / l)
    o = o_all[:, :tq] - lam * o_all[:, tq:]
    ms = jnp.mean(o * o, axis=0, keepdims=True)
    y = o * lax.rsqrt(ms + SUBLN_EPS) * w_ref[...] * (1.0 - lambda_init)
    o_ref[...] = (y.T * _silu(g_ref[...])).astype(BF16)


def _diff_attention(qbt, kb, vbt, gb, strip, subln_w2, lq1, lk1, lq2, lk2, lambda_init):
    bsz, seq, _ = kb.shape
    tq, tk = DIFF_TQ, DIFF_TK
    nk = seq // tk
    assert nk % 2 == 0 and nk >= 2
    lam_spec = pl.BlockSpec((1, HEAD_DIM), lambda b, h, i: (0, 0))
    return pl.pallas_call(
        functools.partial(_diff_kernel, tq=tq, tk=tk, nk=nk, lambda_init=lambda_init),
        grid=(bsz, HB, seq // tq),
        in_specs=[pl.BlockSpec((None, E_B, tq), lambda b, h, i: (b, h, i)),
                  pl.BlockSpec((None, seq, E_B), lambda b, h, i: (b, 0, h)),
                  pl.BlockSpec((None, E_B, seq), lambda b, h, i: (b, h, 0)),
                  pl.BlockSpec((None, 4 * tk, tq), lambda b, h, i: (h, 0, 0)),
                  pl.BlockSpec((None, tq, E_B), lambda b, h, i: (b, i, h)),
                  pl.BlockSpec((E_B, 1), lambda b, h, i: (0, 0)),
                  lam_spec, lam_spec, lam_spec, lam_spec],
        out_specs=pl.BlockSpec((None, tq, E_B), lambda b, h, i: (b, i, h)),
        out_shape=jax.ShapeDtypeStruct((bsz, seq, D_B), BF16),
        scratch_shapes=[pltpu.VMEM((tk, 2 * tq), F32),
                        pltpu.VMEM((tk, 2 * tq), F32),
                        pltpu.VMEM((E_B, 2 * tq), F32)],
        compiler_params=pltpu.CompilerParams(vmem_limit_bytes=VMEM_LIMIT),
        name="diff_attention",
    )(qbt, kb, vbt, strip, gb, subln_w2, lq1, lk1, lq2, lk2)


def _outproj_kernel(oa_ref, ob_ref, x_ref, gate_ref, wa_ref, wb_ref, lng_ref, lnb_ref, y_ref):
    h = jnp.dot(oa_ref[...], wa_ref[...], preferred_element_type=F32)
    h += jnp.dot(ob_ref[...], wb_ref[...], preferred_element_type=F32)
    z = ALPHA * x_ref[...] + gate_ref[...] * h
    mu = jnp.mean(z, axis=-1, keepdims=True)
    zc = z - mu
    var = jnp.mean(zc * zc, axis=-1, keepdims=True)
    y_ref[...] = zc * lax.rsqrt(var + LN_EPS) * lng_ref[...] + lnb_ref[...]


def _out_projection(oa, ob, x, mod4, b_off, w_out_a, w_out_b, ln_g2, ln_b2):
    bsz, seq, _ = x.shape
    tm = ROW_TILE
    row = lambda w: pl.BlockSpec((None, tm, w), lambda b, t: (b, t, 0))
    full = lambda r, c: pl.BlockSpec((r, c), lambda b, t: (0, 0))
    return pl.pallas_call(
        _outproj_kernel,
        grid=(bsz, seq // tm),
        in_specs=[row(D_A), row(D_B), row(D_MODEL),
                  pl.BlockSpec((None, None, 1, D_MODEL), lambda b, t: (b + b_off, 2, 0, 0)),
                  full(D_A, D_MODEL), full(D_B, D_MODEL), full(1, D_MODEL), full(1, D_MODEL)],
        out_specs=row(D_MODEL),
        out_shape=jax.ShapeDtypeStruct((bsz, seq, D_MODEL), F32),
        compiler_params=pltpu.CompilerParams(vmem_limit_bytes=VMEM_LIMIT),
        name="out_projection",
    )(oa, ob, x, mod4, w_out_a, w_out_b, ln_g2, ln_b2)


def _bias_kernel(tab_ref, bucket_ref, o_ref, *, col0):
    h = pl.program_id(0) + col0
    bucket = bucket_ref[...]
    acc = jnp.full(bucket.shape, NEG_INF, F32)
    for b in range(N_BUCKETS):
        acc = jnp.where(bucket == b, tab_ref[b, h] * LOG2E, acc)
    o_ref[...] = acc


def _expand_bias(rel_bias, bucket, col0, n_heads, name):
    rows, cols = bucket.shape
    return pl.pallas_call(
        functools.partial(_bias_kernel, col0=col0),
        grid=(n_heads,),
        in_specs=[pl.BlockSpec(memory_space=pltpu.SMEM),
                  pl.BlockSpec((rows, cols), lambda h: (0, 0))],
        out_specs=pl.BlockSpec((None, rows, cols), lambda h: (h, 0, 0)),
        out_shape=jax.ShapeDtypeStruct((n_heads, rows, cols), F32),
        name=name,
    )(rel_bias, bucket)


def _bias_tables(rel_bias):
    kk = jnp.arange(3 * BLK, dtype=jnp.int32)[:, None]
    a = jnp.arange(BLK, dtype=jnp.int32)[None, :]
    rel = kk - BLK - a
    win_bucket = jnp.where(jnp.abs(rel) <= WINDOW, _t5_bucket(rel), -1)
    win = _expand_bias(rel_bias, win_bucket, 0, HA, "window_bias")
    r = jnp.arange(-2 * DIFF_TK, 2 * DIFF_TK, dtype=jnp.int32)[:, None]
    a = jnp.arange(DIFF_TQ, dtype=jnp.int32)[None, :]
    strip = _expand_bias(rel_bias, _t5_bucket(r - a), HA, HB, "diff_bias")
    return win, strip


def _layer(x, mod4, b_off, w_std, w_fm, w_out_a, w_out_b, ln_g2, ln_b2, sink2, win_bias, strip,
           subln_w2, lq1, lk1, lq2, lk2, lambda_init):
    ka, ga, kb, gb, qat, vat, qbt, vbt = _in_projection(x, mod4, b_off, w_std, w_fm)
    oa = _window_attention(qat, ka, vat, ga, win_bias, sink2)
    ob = _diff_attention(qbt, kb, vbt, gb, strip, subln_w2, lq1, lk1, lq2, lk2, lambda_init)
    return _out_projection(oa, ob, x, mod4, b_off, w_out_a, w_out_b, ln_g2, ln_b2)


def kernel(x_prompt, x_sample, c_prompt, c_sample, w_in, w_out, w_ada, b_ada, ln_g, ln_b, attn_sink,
           lambda_q1, lambda_k1, lambda_q2, lambda_k2, subln_w, rel_bias):
    assert w_in.shape[0] == DEPTH
    nbp, nbs = c_prompt.shape[0], c_sample.shape[0]
    pad = (-(nbp + nbs)) % 16
    c_all = jnp.concatenate([c_prompt, c_sample, jnp.zeros((pad, D_MODEL), F32)], axis=0)
    win_bias, strip = _bias_tables(rel_bias)
    xp, xs = x_prompt, x_sample
    for l in range(DEPTH):
        lambda_init = 0.8 - 0.6 * math.exp(-0.3 * l)
        w = w_in[l]
        sl = lambda n: w[:, _OFF[n][0]:_OFF[n][1]]
        w_std = jnp.concatenate([sl("ka"), sl("ga"), sl("kb"), sl("gb")], axis=1).astype(BF16)
        w_fm = jnp.concatenate([sl("qa"), sl("va"), sl("qb"), sl("vb")], axis=1).T.astype(BF16)
        w_out_a = w_out[l, :D_A].astype(BF16)
        w_out_b = w_out[l, D_A:].astype(BF16)
        mod = _modulation(c_all, w_ada[l], b_ada[l])
        mod4 = mod.reshape(mod.shape[0], 3, 1, D_MODEL)
        args = (w_std, w_fm, w_out_a, w_out_b, ln_g[l].reshape(1, -1), ln_b[l].reshape(1, -1),
                attn_sink[l] * LOG2E, win_bias, strip, subln_w[l].reshape(-1, 1),
                lambda_q1[l].reshape(1, -1), lambda_k1[l].reshape(1, -1),
                lambda_q2[l].reshape(1, -1), lambda_k2[l].reshape(1, -1), lambda_init)
        xp = _layer(xp, mod4, 0, *args)
        xs = _layer(xs, mod4, nbp, *args)
    return (xp, xs)
```

```python
import functools
import math

import jax
import jax.numpy as jnp
from jax import lax
from jax.experimental import pallas as pl
from jax.experimental.pallas import tpu as pltpu

F32 = jnp.float32
BF16 = jnp.bfloat16

D_MODEL = 1024
HEAD_DIM = 64
HA = 8
HKV_A = 2
G_A = HA // HKV_A
HB = 4
E_B = 2 * HEAD_DIM
VB_ROWS = E_B + 16
D_A = HA * HEAD_DIM
D_KV_A = HKV_A * HEAD_DIM
D_B = HB * E_B
WINDOW = 128
BLK = 128
N_BUCKETS = 32
HALF_BUCKETS = N_BUCKETS // 2
MAX_EXACT = HALF_BUCKETS // 2
MAX_DIST = 128
DEPTH = 1
ALPHA = (2.0 * DEPTH) ** 0.25
LN_EPS = 1e-5
SUBLN_EPS = 1e-5
NEG_INF = -1e30
LOG2E = 1.4426950408889634
Q_SCALE = HEAD_DIM ** -0.5 * LOG2E

_OFF = {}
_o = 0
for _name, _w in (("qa", D_A), ("ka", D_KV_A), ("va", D_KV_A), ("ga", D_A),
                  ("qb", D_B), ("kb", D_B), ("vb", D_B), ("gb", D_B)):
    _OFF[_name] = (_o, _o + _w)
    _o += _w

ROW_TILE = 512
DIFF_TQ = 256
DIFF_TK = 512
DIFF_AHEAD = 2
DIFF_BUFFERS = DIFF_AHEAD + 2
STRIP_LEFT = 2 * DIFF_TK
STRIP_RIGHT = -(-(DIFF_TQ + MAX_DIST - 1) // DIFF_TK) * DIFF_TK
STRIP_ROWS = STRIP_LEFT + STRIP_RIGHT + DIFF_TK
VMEM_LIMIT = 56 * 1024 * 1024


def _t5_bucket(rel):
    sign = jnp.where(rel > 0, HALF_BUCKETS, 0)
    n = jnp.abs(rel)
    nf = jnp.maximum(n, 1).astype(jnp.float32)
    large = MAX_EXACT + (jnp.log(nf / MAX_EXACT) / math.log(MAX_DIST / MAX_EXACT)
                         * (HALF_BUCKETS - MAX_EXACT)).astype(jnp.int32)
    large = jnp.minimum(large, HALF_BUCKETS - 1)
    return (sign + jnp.where(n < MAX_EXACT, n, large)).astype(jnp.int32)


def _silu(g):
    return g * jax.nn.sigmoid(g)


def _mod_kernel(c_ref, w_ref, b_ref, o_ref):
    a = _silu(c_ref[...])
    a_hi = a.astype(BF16)
    a_lo = (a - a_hi.astype(F32)).astype(BF16)
    w = w_ref[...]
    w_hi = w.astype(BF16)
    w_lo = (w - w_hi.astype(F32)).astype(BF16)
    acc = jnp.dot(a_hi, w_hi, preferred_element_type=F32)
    acc += jnp.dot(a_lo, w_hi, preferred_element_type=F32)
    acc += jnp.dot(a_hi, w_lo, preferred_element_type=F32)
    o_ref[...] = acc + b_ref[...]


def _modulation(c_all, w_ada, b_ada):
    rows = c_all.shape[0]
    n = w_ada.shape[1]
    tn = 512
    return pl.pallas_call(
        _mod_kernel,
        grid=(n // tn,),
        in_specs=[pl.BlockSpec((rows, D_MODEL), lambda j: (0, 0)),
                  pl.BlockSpec((D_MODEL, tn), lambda j: (0, j)),
                  pl.BlockSpec((1, tn), lambda j: (0, j))],
        out_specs=pl.BlockSpec((rows, tn), lambda j: (0, j)),
        out_shape=jax.ShapeDtypeStruct((rows, n), F32),
        name="modulation",
    )(c_all, w_ada, b_ada.reshape(1, n))


def _inproj_kernel(x_ref, shift_ref, scale_ref, wstd_ref, wfm_ref,
                   ka_ref, ga_ref, kb_ref, gb_ref, qat_ref, vat_ref, qbt_ref, vbt_ref):
    u = (x_ref[...] * (1.0 + scale_ref[...]) + shift_ref[...]).astype(BF16)
    r = jnp.dot(u, wstd_ref[...], preferred_element_type=F32)
    ka_ref[...] = r[:, 0:128].astype(BF16)
    ga_ref[...] = r[:, 128:640]
    kb_ref[...] = r[:, 640:1152].astype(BF16)
    gb_ref[...] = r[:, 1152:1664]
    rt = lax.dot_general(wfm_ref[...], u, (((1,), (1,)), ((), ())),
                         preferred_element_type=F32)
    qat_ref[...] = (rt[0:512] * Q_SCALE).astype(BF16)
    vat_ref[...] = rt[512:640].astype(BF16)
    qbt_ref[...] = (rt[640:1152] * Q_SCALE).astype(BF16)
    ones = jnp.ones((VB_ROWS - E_B, rt.shape[1]), BF16)
    for h in range(HB):
        vbt_ref[h, 0:E_B, :] = rt[1152 + h * E_B:1152 + (h + 1) * E_B].astype(BF16)
        vbt_ref[h, E_B:VB_ROWS, :] = ones


def _in_projection(x, mod4, b_off, w_std, w_fm):
    bsz, seq, _ = x.shape
    tm = ROW_TILE
    n_std = w_std.shape[1]
    n_fm = w_fm.shape[0]
    row = lambda w: pl.BlockSpec((None, tm, w), lambda b, t: (b, t, 0))
    col = lambda h: pl.BlockSpec((None, h, tm), lambda b, t: (b, 0, t))
    vec = lambda k: pl.BlockSpec((None, None, 1, D_MODEL), lambda b, t: (b + b_off, k, 0, 0))
    return pl.pallas_call(
        _inproj_kernel,
        grid=(bsz, seq // tm),
        in_specs=[row(D_MODEL), vec(0), vec(1),
                  pl.BlockSpec((D_MODEL, n_std), lambda b, t: (0, 0)),
                  pl.BlockSpec((n_fm, D_MODEL), lambda b, t: (0, 0))],
        out_specs=[row(D_KV_A), row(D_A), row(D_B), row(D_B),
                   col(D_A), col(D_KV_A), col(D_B),
                   pl.BlockSpec((None, HB, VB_ROWS, tm), lambda b, t: (b, 0, 0, t))],
        out_shape=[jax.ShapeDtypeStruct((bsz, seq, D_KV_A), BF16),
                   jax.ShapeDtypeStruct((bsz, seq, D_A), F32),
                   jax.ShapeDtypeStruct((bsz, seq, D_B), BF16),
                   jax.ShapeDtypeStruct((bsz, seq, D_B), F32),
                   jax.ShapeDtypeStruct((bsz, D_A, seq), BF16),
                   jax.ShapeDtypeStruct((bsz, D_KV_A, seq), BF16),
                   jax.ShapeDtypeStruct((bsz, D_B, seq), BF16),
                   jax.ShapeDtypeStruct((bsz, HB, VB_ROWS, seq), BF16)],
        compiler_params=pltpu.CompilerParams(vmem_limit_bytes=VMEM_LIMIT),
        name="in_projection",
    )(x, mod4, mod4, w_std, w_fm)


def _win_kernel(sink_ref, qt_ref, kl_ref, kc_ref, kr_ref, vl_ref, vc_ref, vr_ref,
                bias_ref, g_ref, o_ref, *, nb):
    i = pl.program_id(1)
    kwin = jnp.concatenate([kl_ref[...], kc_ref[...], kr_ref[...]], axis=0)
    vwin = jnp.concatenate([vl_ref[...], vc_ref[...], vr_ref[...]], axis=1)
    krow = lax.broadcasted_iota(jnp.int32, (3 * BLK, BLK), 0)
    off_l = jnp.where(i == 0, NEG_INF, 0.0).astype(F32)
    off_r = jnp.where(i == nb - 1, NEG_INF, 0.0).astype(F32)
    off = jnp.where(krow < BLK, off_l, 0.0) + jnp.where(krow >= 2 * BLK, off_r, 0.0)
    zeros = jnp.zeros((HEAD_DIM, BLK), BF16)
    outs = []
    for hk in range(HKV_A):
        blocks, biases, sinks = [], [], []
        for g in range(G_A):
            h = hk * G_A + g
            q = qt_ref[h * HEAD_DIM:(h + 1) * HEAD_DIM, :]
            blocks.append(jnp.concatenate([q, zeros] if hk == 0 else [zeros, q], axis=0))
            biases.append(bias_ref[h] + off)
            sinks.append(jnp.full((1, BLK), sink_ref[h], F32))
        q_aug = jnp.concatenate(blocks, axis=1)
        s = jnp.dot(kwin, q_aug, preferred_element_type=F32)
        s = s + jnp.concatenate(biases, axis=1)
        sink = jnp.concatenate(sinks, axis=1)
        m = jnp.maximum(jnp.max(s, axis=0, keepdims=True), sink)
        p = jnp.exp2(s - m)
        den = jnp.sum(p, axis=0, keepdims=True) + jnp.exp2(sink - m)
        ot = jnp.dot(vwin[hk * HEAD_DIM:(hk + 1) * HEAD_DIM, :], p.astype(BF16),
                     preferred_element_type=F32)
        ot = ot * (1.0 / den)
        for g in range(G_A):
            outs.append(ot[:, g * BLK:(g + 1) * BLK])
    o = jnp.concatenate(outs, axis=0).T
    o_ref[...] = (o * _silu(g_ref[...])).astype(BF16)


def _window_attention(qat, ka, vat, ga, bias_t, sink2):
    bsz, seq, _ = ka.shape
    nb = seq // BLK
    kblk = lambda f: pl.BlockSpec((None, BLK, D_KV_A), lambda b, i: (b, f(i), 0))
    vblk = lambda f: pl.BlockSpec((None, D_KV_A, BLK), lambda b, i: (b, 0, f(i)))
    left = lambda i: jnp.maximum(i - 1, 0)
    mid = lambda i: i
    right = lambda i: jnp.minimum(i + 1, nb - 1)
    return pl.pallas_call(
        functools.partial(_win_kernel, nb=nb),
        grid=(bsz, nb),
        in_specs=[pl.BlockSpec(memory_space=pltpu.SMEM),
                  pl.BlockSpec((None, D_A, BLK), lambda b, i: (b, 0, i)),
                  kblk(left), kblk(mid), kblk(right),
                  vblk(left), vblk(mid), vblk(right),
                  pl.BlockSpec((HA, 3 * BLK, BLK), lambda b, i: (0, 0, 0)),
                  pl.BlockSpec((None, BLK, D_A), lambda b, i: (b, i, 0))],
        out_specs=pl.BlockSpec((None, BLK, D_A), lambda b, i: (b, i, 0)),
        out_shape=jax.ShapeDtypeStruct((bsz, seq, D_A), BF16),
        compiler_params=pltpu.CompilerParams(vmem_limit_bytes=VMEM_LIMIT),
        name="window_attention",
    )(sink2, qat, ka, ka, ka, vat, vat, vat, bias_t, ga)


def _diff_kernel(zero_ref, qt_ref, k_ref, vt_ref, strip_ref, g_ref, w_ref, lq1_ref, lk1_ref, lq2_ref,
                 lk2_ref, o_ref, acc_sc, *s_scs, tq, tk, nk, lambda_init):
    i = pl.program_id(2)
    st_rows = ld_rows = pl.ds(pl.multiple_of(zero_ref[0], 8), tk)
    qt = qt_ref[...]
    qrow = lax.broadcasted_iota(jnp.int32, (E_B, tq), 0)
    zero = jnp.zeros_like(qt)
    q_aug = jnp.concatenate([jnp.where(qrow < HEAD_DIM, qt, zero),
                             jnp.where(qrow >= HEAD_DIM, qt, zero)], axis=1)
    acc_sc[...] = jnp.zeros(acc_sc.shape, F32)

    i0 = lax.div(i * tq, tk)
    general_visits = (0, 1, nk - 1)
    c_left = jnp.concatenate([strip_ref[0:1, :]] * 2, axis=1)
    c_right = jnp.concatenate([strip_ref[STRIP_ROWS - 1:STRIP_ROWS, :]] * 2, axis=1)

    def key_tile(jj):
        j = i0 + jj
        wrapped = j >= nk
        return jnp.where(wrapped, j - nk, j), wrapped

    def logits(jj, s_sc):
        j, _ = key_tile(jj)
        k0 = pl.multiple_of(j * tk, tk)
        s_sc[st_rows, :] = jnp.dot(k_ref[pl.ds(k0, tk), :], q_aug, preferred_element_type=F32)

    def accumulate(jj, s_sc, m_prev):
        j, wrapped = key_tile(jj)
        k0 = pl.multiple_of(j * tk, tk)
        if jj in general_visits:
            r0 = jnp.clip(j * tk - i * tq, -STRIP_LEFT, STRIP_RIGHT) + STRIP_LEFT
            bias = strip_ref[pl.ds(pl.multiple_of(r0, 128), tk), :]
            s_sc[st_rows, :] = s_sc[ld_rows, :] + jnp.concatenate([bias, bias], axis=1)
            m_new = jnp.maximum(m_prev, jnp.max(s_sc[ld_rows, :], axis=0, keepdims=True))
            shift = m_new
        else:
            c = jnp.where(wrapped, c_left, c_right)
            m_new = jnp.maximum(m_prev, jnp.max(s_sc[ld_rows, :], axis=0, keepdims=True) + c)
            shift = m_new - c
        alpha = jnp.exp2(m_prev - m_new)
        pv = None
        for c0 in range(0, tk, 256):
            rows = pl.ds(pl.multiple_of(zero_ref[0] + c0, 8), 256)
            p = jnp.exp2((s_sc[rows, :] - shift).astype(BF16))
            part = jnp.dot(vt_ref[:, pl.ds(pl.multiple_of(k0 + c0, 256), 256)], p,
                           preferred_element_type=F32)
            pv = part if pv is None else pv + part
        acc_sc[...] = acc_sc[...] * alpha + pv
        return m_new

    nbuf = len(s_scs)
    ahead = DIFF_AHEAD
    m = jnp.full((1, 2 * tq), NEG_INF, F32)
    for jj in range(ahead):
        logits(jj, s_scs[jj])
    for jj in range(nk):
        if jj + ahead < nk:
            logits(jj + ahead, s_scs[(jj + ahead) % nbuf])
        m = accumulate(jj, s_scs[jj % nbuf], m)

    lam = (jnp.exp(jnp.sum(lq1_ref[...] * lk1_ref[...], axis=1, keepdims=True))
           - jnp.exp(jnp.sum(lq2_ref[...] * lk2_ref[...], axis=1, keepdims=True)) + lambda_init)
    o_all = acc_sc[0:E_B, :] * (1.0 / acc_sc[E_B:E_B + 1, :])
    o = o_all[:, :tq] - lam * o_all[:, tq:]
    ms = jnp.mean(o * o, axis=0, keepdims=True)
    y = o * lax.rsqrt(ms + SUBLN_EPS) * w_ref[...] * (1.0 - lambda_init)
    o_ref[...] = (y.T * _silu(g_ref[...])).astype(BF16)


def _diff_attention(qbt, kb, vbt, gb, strip, subln_w2, lq1, lk1, lq2, lk2, lambda_init):
    bsz, seq, _ = kb.shape
    tq, tk = DIFF_TQ, DIFF_TK
    nk = seq // tk
    assert nk > DIFF_AHEAD
    lam_spec = pl.BlockSpec((1, HEAD_DIM), lambda b, h, i: (0, 0))
    return pl.pallas_call(
        functools.partial(_diff_kernel, tq=tq, tk=tk, nk=nk, lambda_init=lambda_init),
        grid=(bsz, HB, seq // tq),
        in_specs=[pl.BlockSpec(memory_space=pltpu.SMEM),
                  pl.BlockSpec((None, E_B, tq), lambda b, h, i: (b, h, i)),
                  pl.BlockSpec((None, seq, E_B), lambda b, h, i: (b, 0, h)),
                  pl.BlockSpec((None, None, VB_ROWS, seq), lambda b, h, i: (b, h, 0, 0)),
                  pl.BlockSpec((None, STRIP_ROWS, tq), lambda b, h, i: (h, 0, 0)),
                  pl.BlockSpec((None, tq, E_B), lambda b, h, i: (b, i, h)),
                  pl.BlockSpec((E_B, 1), lambda b, h, i: (0, 0)),
                  lam_spec, lam_spec, lam_spec, lam_spec],
        out_specs=pl.BlockSpec((None, tq, E_B), lambda b, h, i: (b, i, h)),
        out_shape=jax.ShapeDtypeStruct((bsz, seq, D_B), BF16),
        scratch_shapes=[pltpu.VMEM((VB_ROWS, 2 * tq), F32)]
                       + [pltpu.VMEM((tk + 8, 2 * tq), F32)] * DIFF_BUFFERS,
        compiler_params=pltpu.CompilerParams(vmem_limit_bytes=VMEM_LIMIT),
        name="diff_attention",
    )(jnp.zeros((1,), jnp.int32), qbt, kb, vbt, strip, gb, subln_w2, lq1, lk1, lq2, lk2)


def _outproj_kernel(oa_ref, ob_ref, x_ref, gate_ref, wa_ref, wb_ref, lng_ref, lnb_ref, y_ref):
    h = jnp.dot(oa_ref[...], wa_ref[...], preferred_element_type=F32)
    h += jnp.dot(ob_ref[...], wb_ref[...], preferred_element_type=F32)
    z = ALPHA * x_ref[...] + gate_ref[...] * h
    mu = jnp.mean(z, axis=-1, keepdims=True)
    zc = z - mu
    var = jnp.mean(zc * zc, axis=-1, keepdims=True)
    y_ref[...] = zc * lax.rsqrt(var + LN_EPS) * lng_ref[...] + lnb_ref[...]


def _out_projection(oa, ob, x, mod4, b_off, w_out_a, w_out_b, ln_g2, ln_b2):
    bsz, seq, _ = x.shape
    tm = ROW_TILE
    row = lambda w: pl.BlockSpec((None, tm, w), lambda b, t: (b, t, 0))
    full = lambda r, c: pl.BlockSpec((r, c), lambda b, t: (0, 0))
    return pl.pallas_call(
        _outproj_kernel,
        grid=(bsz, seq // tm),
        in_specs=[row(D_A), row(D_B), row(D_MODEL),
                  pl.BlockSpec((None, None, 1, D_MODEL), lambda b, t: (b + b_off, 2, 0, 0)),
                  full(D_A, D_MODEL), full(D_B, D_MODEL), full(1, D_MODEL), full(1, D_MODEL)],
        out_specs=row(D_MODEL),
        out_shape=jax.ShapeDtypeStruct((bsz, seq, D_MODEL), F32),
        compiler_params=pltpu.CompilerParams(vmem_limit_bytes=VMEM_LIMIT),
        name="out_projection",
    )(oa, ob, x, mod4, w_out_a, w_out_b, ln_g2, ln_b2)


def _bias_kernel(tab_ref, bucket_ref, o_ref, *, col0):
    h = pl.program_id(0) + col0
    bucket = bucket_ref[...]
    acc = jnp.full(bucket.shape, NEG_INF, F32)
    for b in range(N_BUCKETS):
        acc = jnp.where(bucket == b, tab_ref[b, h] * LOG2E, acc)
    o_ref[...] = acc


def _expand_bias(rel_bias, bucket, col0, n_heads, name):
    rows, cols = bucket.shape
    return pl.pallas_call(
        functools.partial(_bias_kernel, col0=col0),
        grid=(n_heads,),
        in_specs=[pl.BlockSpec(memory_space=pltpu.SMEM),
                  pl.BlockSpec((rows, cols), lambda h: (0, 0))],
        out_specs=pl.BlockSpec((None, rows, cols), lambda h: (h, 0, 0)),
        out_shape=jax.ShapeDtypeStruct((n_heads, rows, cols), F32),
        name=name,
    )(rel_bias, bucket)


def _bias_tables(rel_bias):
    kk = jnp.arange(3 * BLK, dtype=jnp.int32)[:, None]
    a = jnp.arange(BLK, dtype=jnp.int32)[None, :]
    rel = kk - BLK - a
    win_bucket = jnp.where(jnp.abs(rel) <= WINDOW, _t5_bucket(rel), -1)
    win = _expand_bias(rel_bias, win_bucket, 0, HA, "window_bias")
    r = jnp.arange(-STRIP_LEFT, STRIP_RIGHT + DIFF_TK, dtype=jnp.int32)[:, None]
    a = jnp.arange(DIFF_TQ, dtype=jnp.int32)[None, :]
    strip = _expand_bias(rel_bias, _t5_bucket(r - a), HA, HB, "diff_bias")
    return win, strip


def _layer(x, mod4, b_off, w_std, w_fm, w_out_a, w_out_b, ln_g2, ln_b2, sink2, win_bias, strip,
           subln_w2, lq1, lk1, lq2, lk2, lambda_init):
    ka, ga, kb, gb, qat, vat, qbt, vbt = _in_projection(x, mod4, b_off, w_std, w_fm)
    oa = _window_attention(qat, ka, vat, ga, win_bias, sink2)
    ob = _diff_attention(qbt, kb, vbt, gb, strip, subln_w2, lq1, lk1, lq2, lk2, lambda_init)
    return _out_projection(oa, ob, x, mod4, b_off, w_out_a, w_out_b, ln_g2, ln_b2)


def kernel(x_prompt, x_sample, c_prompt, c_sample, w_in, w_out, w_ada, b_ada, ln_g, ln_b, attn_sink,
           lambda_q1, lambda_k1, lambda_q2, lambda_k2, subln_w, rel_bias):
    assert w_in.shape[0] == DEPTH
    nbp, nbs = c_prompt.shape[0], c_sample.shape[0]
    pad = (-(nbp + nbs)) % 16
    c_all = jnp.concatenate([c_prompt, c_sample, jnp.zeros((pad, D_MODEL), F32)], axis=0)
    win_bias, strip = _bias_tables(rel_bias)
    xp, xs = x_prompt, x_sample
    for l in range(DEPTH):
        lambda_init = 0.8 - 0.6 * math.exp(-0.3 * l)
        w = w_in[l]
        sl = lambda n: w[:, _OFF[n][0]:_OFF[n][1]]
        w_std = jnp.concatenate([sl("ka"), sl("ga"), sl("kb"), sl("gb")], axis=1).astype(BF16)
        w_fm = jnp.concatenate([sl("qa"), sl("va"), sl("qb"), sl("vb")], axis=1).T.astype(BF16)
        w_out_a = w_out[l, :D_A].astype(BF16)
        w_out_b = w_out[l, D_A:].astype(BF16)
        mod = _modulation(c_all, w_ada[l], b_ada[l])
        mod4 = mod.reshape(mod.shape[0], 3, 1, D_MODEL)
        args = (w_std, w_fm, w_out_a, w_out_b, ln_g[l].reshape(1, -1), ln_b[l].reshape(1, -1),
                attn_sink[l] * LOG2E, win_bias, strip, subln_w[l].reshape(-1, 1),
                lambda_q1[l].reshape(1, -1), lambda_k1[l].reshape(1, -1),
                lambda_q2[l].reshape(1, -1), lambda_k2[l].reshape(1, -1), lambda_init)
        xp = _layer(xp, mod4, 0, *args)
        xs = _layer(xs, mod4, nbp, *args)
    return (xp, xs)
```

```python
import functools
import math

import jax
import jax.numpy as jnp
from jax import lax
from jax.experimental import pallas as pl
from jax.experimental.pallas import tpu as pltpu

F32 = jnp.float32
BF16 = jnp.bfloat16

D_MODEL = 1024
HEAD_DIM = 64
HA = 8
HKV_A = 2
G_A = HA // HKV_A
HB = 4
E_B = 2 * HEAD_DIM
ONES_ROWS = 16
VB_ROWS = E_B + ONES_ROWS
VA_ROWS = HEAD_DIM + ONES_ROWS
D_A = HA * HEAD_DIM
D_KV_A = HKV_A * HEAD_DIM
D_B = HB * E_B
WINDOW = 128
BLK = 128
N_BUCKETS = 32
HALF_BUCKETS = N_BUCKETS // 2
MAX_EXACT = HALF_BUCKETS // 2
MAX_DIST = 128
DEPTH = 1
ALPHA = (2.0 * DEPTH) ** 0.25
LN_EPS = 1e-5
SUBLN_EPS = 1e-5
NEG_INF = -1e30
LOG2E = 1.4426950408889634
Q_SCALE = HEAD_DIM ** -0.5 * LOG2E

_OFF = {}
_o = 0
for _name, _w in (("qa", D_A), ("ka", D_KV_A), ("va", D_KV_A), ("ga", D_A),
                  ("qb", D_B), ("kb", D_B), ("vb", D_B), ("gb", D_B)):
    _OFF[_name] = (_o, _o + _w)
    _o += _w

ROW_TILE = 512
WIN_BLOCKS = 4
WIN_AHEAD = 2
DIFF_TQ = 256
DIFF_SUBTILES = 2
DIFF_TK = 512
DIFF_AHEAD = 2
DIFF_BUFFERS = DIFF_AHEAD + 2
STRIP_LEFT = 2 * DIFF_TK
STRIP_RIGHT = -(-(DIFF_TQ + MAX_DIST - 1) // DIFF_TK) * DIFF_TK
STRIP_ROWS = STRIP_LEFT + STRIP_RIGHT + DIFF_TK
VMEM_LIMIT = 56 * 1024 * 1024


def _t5_bucket(rel):
    sign = jnp.where(rel > 0, HALF_BUCKETS, 0)
    n = jnp.abs(rel)
    nf = jnp.maximum(n, 1).astype(jnp.float32)
    large = MAX_EXACT + (jnp.log(nf / MAX_EXACT) / math.log(MAX_DIST / MAX_EXACT)
                         * (HALF_BUCKETS - MAX_EXACT)).astype(jnp.int32)
    large = jnp.minimum(large, HALF_BUCKETS - 1)
    return (sign + jnp.where(n < MAX_EXACT, n, large)).astype(jnp.int32)


def _silu(g):
    return g * jax.nn.sigmoid(g)


def _mod_kernel(c_ref, w_ref, b_ref, o_ref):
    a = _silu(c_ref[...])
    a_hi = a.astype(BF16)
    a_lo = (a - a_hi.astype(F32)).astype(BF16)
    w = w_ref[...]
    w_hi = w.astype(BF16)
    w_lo = (w - w_hi.astype(F32)).astype(BF16)
    acc = jnp.dot(a_hi, w_hi, preferred_element_type=F32)
    acc += jnp.dot(a_lo, w_hi, preferred_element_type=F32)
    acc += jnp.dot(a_hi, w_lo, preferred_element_type=F32)
    o_ref[...] = acc + b_ref[...]


def _modulation(c_all, w_ada, b_ada):
    rows = c_all.shape[0]
    n = w_ada.shape[1]
    tn = 512
    return pl.pallas_call(
        _mod_kernel,
        grid=(n // tn,),
        in_specs=[pl.BlockSpec((rows, D_MODEL), lambda j: (0, 0)),
                  pl.BlockSpec((D_MODEL, tn), lambda j: (0, j)),
                  pl.BlockSpec((1, tn), lambda j: (0, j))],
        out_specs=pl.BlockSpec((rows, tn), lambda j: (0, j)),
        out_shape=jax.ShapeDtypeStruct((rows, n), F32),
        name="modulation",
    )(c_all, w_ada, b_ada.reshape(1, n))


def _inproj_kernel(x_ref, shift_ref, scale_ref, wstd_ref, wfm_ref,
                   ka_ref, ga_ref, kb_ref, gb_ref, qat_ref, vat_ref, qbt_ref, vbt_ref):
    u = (x_ref[...] * (1.0 + scale_ref[...]) + shift_ref[...]).astype(BF16)
    r = jnp.dot(u, wstd_ref[...], preferred_element_type=F32)
    ka_ref[...] = r[:, 0:128].astype(BF16)
    ga_ref[...] = r[:, 128:640]
    kb_ref[...] = r[:, 640:1152].astype(BF16)
    gb_ref[...] = r[:, 1152:1664]
    rt = lax.dot_general(wfm_ref[...], u, (((1,), (1,)), ((), ())),
                         preferred_element_type=F32)
    qat_ref[...] = (rt[0:512] * Q_SCALE).astype(BF16)
    qbt_ref[...] = (rt[640:1152] * Q_SCALE).astype(BF16)
    ones = jnp.ones((ONES_ROWS, rt.shape[1]), BF16)
    for h in range(HKV_A):
        vat_ref[h, 0:HEAD_DIM, :] = rt[512 + h * HEAD_DIM:512 + (h + 1) * HEAD_DIM].astype(BF16)
        vat_ref[h, HEAD_DIM:VA_ROWS, :] = ones
    for h in range(HB):
        vbt_ref[h, 0:E_B, :] = rt[1152 + h * E_B:1152 + (h + 1) * E_B].astype(BF16)
        vbt_ref[h, E_B:VB_ROWS, :] = ones


def _in_projection(x, mod4, b_off, w_std, w_fm):
    bsz, seq, _ = x.shape
    tm = ROW_TILE
    n_std = w_std.shape[1]
    n_fm = w_fm.shape[0]
    row = lambda w: pl.BlockSpec((None, tm, w), lambda b, t: (b, t, 0))
    col = lambda h: pl.BlockSpec((None, h, tm), lambda b, t: (b, 0, t))
    vec = lambda k: pl.BlockSpec((None, None, 1, D_MODEL), lambda b, t: (b + b_off, k, 0, 0))
    return pl.pallas_call(
        _inproj_kernel,
        grid=(bsz, seq // tm),
        in_specs=[row(D_MODEL), vec(0), vec(1),
                  pl.BlockSpec((D_MODEL, n_std), lambda b, t: (0, 0)),
                  pl.BlockSpec((n_fm, D_MODEL), lambda b, t: (0, 0))],
        out_specs=[row(D_KV_A), row(D_A), row(D_B), row(D_B),
                   col(D_A),
                   pl.BlockSpec((None, HKV_A, VA_ROWS, tm), lambda b, t: (b, 0, 0, t)),
                   col(D_B),
                   pl.BlockSpec((None, HB, VB_ROWS, tm), lambda b, t: (b, 0, 0, t))],
        out_shape=[jax.ShapeDtypeStruct((bsz, seq, D_KV_A), BF16),
                   jax.ShapeDtypeStruct((bsz, seq, D_A), F32),
                   jax.ShapeDtypeStruct((bsz, seq, D_B), BF16),
                   jax.ShapeDtypeStruct((bsz, seq, D_B), F32),
                   jax.ShapeDtypeStruct((bsz, D_A, seq), BF16),
                   jax.ShapeDtypeStruct((bsz, HKV_A, VA_ROWS, seq), BF16),
                   jax.ShapeDtypeStruct((bsz, D_B, seq), BF16),
                   jax.ShapeDtypeStruct((bsz, HB, VB_ROWS, seq), BF16)],
        compiler_params=pltpu.CompilerParams(vmem_limit_bytes=VMEM_LIMIT),
        name="in_projection",
    )(x, mod4, mod4, w_std, w_fm)


def _win_kernel(zero_ref, sink_ref, qt_ref, kl_ref, kc_ref, kr_ref, vl_ref, vc_ref, vr_ref,
                bias_ref, g_ref, o_ref, *s_scs, n_steps, wb):
    i = pl.program_id(1)
    kall = jnp.concatenate([kl_ref[...], kc_ref[...], kr_ref[...]], axis=0)
    krow = lax.broadcasted_iota(jnp.int32, (3 * BLK, BLK), 0)
    off_l = jnp.where(krow < BLK, jnp.where(i == 0, NEG_INF, 0.0).astype(F32), 0.0)
    off_r = jnp.where(krow >= 2 * BLK, jnp.where(i == n_steps - 1, NEG_INF, 0.0).astype(F32), 0.0)
    zeros = jnp.zeros((HEAD_DIM, BLK), BF16)
    vall = [jnp.concatenate([vl_ref[hk], vc_ref[hk], vr_ref[hk]], axis=1) for hk in range(HKV_A)]
    units = [(blk, hk) for blk in range(wb) for hk in range(HKV_A)]

    def logits(u):
        blk, hk = units[u]
        blocks = []
        for g in range(G_A):
            h = hk * G_A + g
            q = qt_ref[h * HEAD_DIM:(h + 1) * HEAD_DIM, blk * BLK:(blk + 1) * BLK]
            blocks.append(jnp.concatenate([q, zeros] if hk == 0 else [zeros, q], axis=0))
        q_aug = jnp.concatenate(blocks, axis=1)
        return jnp.dot(kall[blk * BLK:(blk + 3) * BLK], q_aug, preferred_element_type=F32)

    def softmax_pv(u, s_sc):
        blk, hk = units[u]
        biases, sinks = [], []
        for g in range(G_A):
            h = hk * G_A + g
            bias = bias_ref[h]
            if blk == 0:
                bias = bias + off_l
            if blk == wb - 1:
                bias = bias + off_r
            biases.append(bias)
            sinks.append(jnp.full((1, BLK), sink_ref[h], F32))
        s_sc[rows_all, :] = s_sc[rows_all, :] + jnp.concatenate(biases, axis=1)
        sink = jnp.concatenate(sinks, axis=1)
        m = jnp.maximum(jnp.max(s_sc[rows_all, :], axis=0, keepdims=True), sink)
        pv = None
        for c0, cn in ((0, 2 * BLK), (2 * BLK, BLK)):
            rows = pl.ds(pl.multiple_of(zero_ref[0] + c0, 8), cn)
            p = jnp.exp2((s_sc[rows, :] - m).astype(BF16))
            part = jnp.dot(vall[hk][:, blk * BLK + c0:blk * BLK + c0 + cn], p,
                           preferred_element_type=F32)
            pv = part if pv is None else pv + part
        den = pv[HEAD_DIM:HEAD_DIM + 1] + jnp.exp2(sink - m)
        ot = pv[0:HEAD_DIM] * (1.0 / den)
        return [ot[:, g * BLK:(g + 1) * BLK] for g in range(G_A)]

    rows_all = pl.ds(pl.multiple_of(zero_ref[0], 8), 3 * BLK)
    nbuf = len(s_scs)
    for u in range(WIN_AHEAD):
        s_scs[u % nbuf][rows_all, :] = logits(u)
    outs = []
    for u, (blk, hk) in enumerate(units):
        if u + WIN_AHEAD < len(units):
            s_scs[(u + WIN_AHEAD) % nbuf][rows_all, :] = logits(u + WIN_AHEAD)
        outs += softmax_pv(u, s_scs[u % nbuf])
        if hk == HKV_A - 1:
            o = jnp.concatenate(outs, axis=0).T
            outs = []
            rows = slice(blk * BLK, (blk + 1) * BLK)
            o_ref[rows, :] = (o * _silu(g_ref[rows, :])).astype(BF16)


def _window_attention(qat, ka, vat, ga, bias_t, sink2):
    bsz, seq, _ = ka.shape
    nb = seq // BLK
    wb = WIN_BLOCKS
    n_steps = nb // wb
    kblk = lambda r, f: pl.BlockSpec((None, r, D_KV_A), lambda b, i: (b, f(i), 0))
    vblk = lambda r, f: pl.BlockSpec((None, HKV_A, VA_ROWS, r), lambda b, i: (b, 0, 0, f(i)))
    left = lambda i: jnp.maximum(i * wb - 1, 0)
    mid = lambda i: i
    right = lambda i: jnp.minimum((i + 1) * wb, nb - 1)
    return pl.pallas_call(
        functools.partial(_win_kernel, n_steps=n_steps, wb=wb),
        grid=(bsz, n_steps),
        in_specs=[pl.BlockSpec(memory_space=pltpu.SMEM),
                  pl.BlockSpec(memory_space=pltpu.SMEM),
                  pl.BlockSpec((None, D_A, wb * BLK), lambda b, i: (b, 0, i)),
                  kblk(BLK, left), kblk(wb * BLK, mid), kblk(BLK, right),
                  vblk(BLK, left), vblk(wb * BLK, mid), vblk(BLK, right),
                  pl.BlockSpec((HA, 3 * BLK, BLK), lambda b, i: (0, 0, 0)),
                  pl.BlockSpec((None, wb * BLK, D_A), lambda b, i: (b, i, 0))],
        out_specs=pl.BlockSpec((None, wb * BLK, D_A), lambda b, i: (b, i, 0)),
        out_shape=jax.ShapeDtypeStruct((bsz, seq, D_A), BF16),
        scratch_shapes=[pltpu.VMEM((3 * BLK + 8, G_A * BLK), F32)] * (WIN_AHEAD + 2),
        compiler_params=pltpu.CompilerParams(vmem_limit_bytes=VMEM_LIMIT),
        name="window_attention",
    )(jnp.zeros((1,), jnp.int32), sink2, qat, ka, ka, ka, vat, vat, vat, bias_t, ga)


def _diff_kernel(zero_ref, qt_ref, k_ref, vt_ref, strip_ref, g_ref, w_ref, lq1_ref, lk1_ref, lq2_ref,
                 lk2_ref, o_ref, *scratch, tq, tk, nk, n_sub, lambda_init):
    acc_scs, s_scs = scratch[:n_sub], scratch[n_sub:]
    st_rows = ld_rows = pl.ds(pl.multiple_of(zero_ref[0], 8), tk)
    c_left = jnp.concatenate([strip_ref[0:1, :]] * 2, axis=1)
    c_right = jnp.concatenate([strip_ref[STRIP_ROWS - 1:STRIP_ROWS, :]] * 2, axis=1)
    qrow = lax.broadcasted_iota(jnp.int32, (E_B, tq), 0)
    general_visits = (0, 1, nk - 1)

    def q_tile(sub):
        qt = qt_ref[:, sub * tq:(sub + 1) * tq]
        zero = jnp.zeros_like(qt)
        q_aug = jnp.concatenate([jnp.where(qrow < HEAD_DIM, qt, zero),
                                 jnp.where(qrow >= HEAD_DIM, qt, zero)], axis=1)
        return pl.program_id(2) * n_sub + sub, q_aug

    def key_tile(i, jj):
        j = lax.div(i * tq, tk) + jj
        wrapped = j >= nk
        return jnp.where(wrapped, j - nk, j), wrapped

    def logits(i, q_aug, jj, s_sc):
        j, _ = key_tile(i, jj)
        k0 = pl.multiple_of(j * tk, tk)
        s_sc[st_rows, :] = jnp.dot(k_ref[pl.ds(k0, tk), :], q_aug, preferred_element_type=F32)

    def accumulate(i, jj, s_sc, acc_sc, m_prev):
        j, wrapped = key_tile(i, jj)
        k0 = pl.multiple_of(j * tk, tk)
        if jj in general_visits:
            r0 = jnp.clip(j * tk - i * tq, -STRIP_LEFT, STRIP_RIGHT) + STRIP_LEFT
            bias = strip_ref[pl.ds(pl.multiple_of(r0, 128), tk), :]
            s_sc[st_rows, :] = s_sc[ld_rows, :] + jnp.concatenate([bias, bias], axis=1)
            m_new = jnp.maximum(m_prev, jnp.max(s_sc[ld_rows, :], axis=0, keepdims=True))
            shift = m_new
        else:
            c = jnp.where(wrapped, c_left, c_right)
            m_new = jnp.maximum(m_prev, jnp.max(s_sc[ld_rows, :], axis=0, keepdims=True) + c)
            shift = m_new - c
        alpha = jnp.exp2(m_prev - m_new)
        pv = None
        for c0 in range(0, tk, 256):
            rows = pl.ds(pl.multiple_of(zero_ref[0] + c0, 8), 256)
            p = jnp.exp2((s_sc[rows, :] - shift).astype(BF16))
            part = jnp.dot(vt_ref[:, pl.ds(pl.multiple_of(k0 + c0, 256), 256)], p,
                           preferred_element_type=F32)
            pv = part if pv is None else pv + part
        if jj == 0:
            acc_sc[...] = pv
        else:
            acc_sc[...] = acc_sc[...] * alpha + pv
        return m_new

    def finish(sub, acc_sc):
        lam = (jnp.exp(jnp.sum(lq1_ref[...] * lk1_ref[...], axis=1, keepdims=True))
               - jnp.exp(jnp.sum(lq2_ref[...] * lk2_ref[...], axis=1, keepdims=True)) + lambda_init)
        o_all = acc_sc[0:E_B, :] * (1.0 / acc_sc[E_B:E_B + 1, :])
        o = o_all[:, :tq] - lam * o_all[:, tq:]
        ms = jnp.mean(o * o, axis=0, keepdims=True)
        y = o * lax.rsqrt(ms + SUBLN_EPS) * w_ref[...] * (1.0 - lambda_init)
        rows = slice(sub * tq, (sub + 1) * tq)
        o_ref[rows, :] = (y.T * _silu(g_ref[rows, :])).astype(BF16)

    nbuf = len(s_scs)
    tiles = [q_tile(sub) for sub in range(n_sub)]
    visits = [(sub, jj) for sub in range(n_sub) for jj in range(nk)]

    def issue_logits(v):
        sub, jj = visits[v]
        logits(*tiles[sub], jj, s_scs[v % nbuf])

    for v in range(DIFF_AHEAD):
        issue_logits(v)
    m = None
    for v, (sub, jj) in enumerate(visits):
        if v + DIFF_AHEAD < len(visits):
            issue_logits(v + DIFF_AHEAD)
        if jj == 0:
            m = jnp.full((1, 2 * tq), NEG_INF, F32)
        m = accumulate(tiles[sub][0], jj, s_scs[v % nbuf], acc_scs[sub], m)
        if jj == nk - 1:
            finish(sub, acc_scs[sub])


def _diff_attention(qbt, kb, vbt, gb, strip, subln_w2, lq1, lk1, lq2, lk2, lambda_init):
    bsz, seq, _ = kb.shape
    tq, tk, n_sub = DIFF_TQ, DIFF_TK, DIFF_SUBTILES
    nk = seq // tk
    assert nk > DIFF_AHEAD and nk >= 3
    tq_step = tq * n_sub
    lam_spec = pl.BlockSpec((1, HEAD_DIM), lambda b, h, i: (0, 0))
    return pl.pallas_call(
        functools.partial(_diff_kernel, tq=tq, tk=tk, nk=nk, n_sub=n_sub, lambda_init=lambda_init),
        grid=(bsz, HB, seq // tq_step),
        in_specs=[pl.BlockSpec(memory_space=pltpu.SMEM),
                  pl.BlockSpec((None, E_B, tq_step), lambda b, h, i: (b, h, i)),
                  pl.BlockSpec((None, seq, E_B), lambda b, h, i: (b, 0, h)),
                  pl.BlockSpec((None, None, VB_ROWS, seq), lambda b, h, i: (b, h, 0, 0)),
                  pl.BlockSpec((None, STRIP_ROWS, tq), lambda b, h, i: (h, 0, 0)),
                  pl.BlockSpec((None, tq_step, E_B), lambda b, h, i: (b, i, h)),
                  pl.BlockSpec((E_B, 1), lambda b, h, i: (0, 0)),
                  lam_spec, lam_spec, lam_spec, lam_spec],
        out_specs=pl.BlockSpec((None, tq_step, E_B), lambda b, h, i: (b, i, h)),
        out_shape=jax.ShapeDtypeStruct((bsz, seq, D_B), BF16),
        scratch_shapes=[pltpu.VMEM((VB_ROWS, 2 * tq), F32)] * n_sub
                       + [pltpu.VMEM((tk + 8, 2 * tq), F32)] * DIFF_BUFFERS,
        compiler_params=pltpu.CompilerParams(vmem_limit_bytes=VMEM_LIMIT),
        name="diff_attention",
    )(jnp.zeros((1,), jnp.int32), qbt, kb, vbt, strip, gb, subln_w2, lq1, lk1, lq2, lk2)


def _outproj_kernel(oa_ref, ob_ref, x_ref, gate_ref, wa_ref, wb_ref, lng_ref, lnb_ref, y_ref):
    h = jnp.dot(oa_ref[...], wa_ref[...], preferred_element_type=F32)
    h += jnp.dot(ob_ref[...], wb_ref[...], preferred_element_type=F32)
    z = ALPHA * x_ref[...] + gate_ref[...] * h
    mu = jnp.mean(z, axis=-1, keepdims=True)
    zc = z - mu
    var = jnp.mean(zc * zc, axis=-1, keepdims=True)
    y_ref[...] = zc * lax.rsqrt(var + LN_EPS) * lng_ref[...] + lnb_ref[...]


def _out_projection(oa, ob, x, mod4, b_off, w_out_a, w_out_b, ln_g2, ln_b2):
    bsz, seq, _ = x.shape
    tm = ROW_TILE
    row = lambda w: pl.BlockSpec((None, tm, w), lambda b, t: (b, t, 0))
    full = lambda r, c: pl.BlockSpec((r, c), lambda b, t: (0, 0))
    return pl.pallas_call(
        _outproj_kernel,
        grid=(bsz, seq // tm),
        in_specs=[row(D_A), row(D_B), row(D_MODEL),
                  pl.BlockSpec((None, None, 1, D_MODEL), lambda b, t: (b + b_off, 2, 0, 0)),
                  full(D_A, D_MODEL), full(D_B, D_MODEL), full(1, D_MODEL), full(1, D_MODEL)],
        out_specs=row(D_MODEL),
        out_shape=jax.ShapeDtypeStruct((bsz, seq, D_MODEL), F32),
        compiler_params=pltpu.CompilerParams(vmem_limit_bytes=VMEM_LIMIT),
        name="out_projection",
    )(oa, ob, x, mod4, w_out_a, w_out_b, ln_g2, ln_b2)


def _bias_kernel(tab_ref, bucket_ref, o_ref, *, col0):
    h = pl.program_id(0) + col0
    bucket = bucket_ref[...]
    acc = jnp.full(bucket.shape, NEG_INF, F32)
    for b in range(N_BUCKETS):
        acc = jnp.where(bucket == b, tab_ref[b, h] * LOG2E, acc)
    o_ref[...] = acc


def _expand_bias(rel_bias, bucket, col0, n_heads, name):
    rows, cols = bucket.shape
    return pl.pallas_call(
        functools.partial(_bias_kernel, col0=col0),
        grid=(n_heads,),
        in_specs=[pl.BlockSpec(memory_space=pltpu.SMEM),
                  pl.BlockSpec((rows, cols), lambda h: (0, 0))],
        out_specs=pl.BlockSpec((None, rows, cols), lambda h: (h, 0, 0)),
        out_shape=jax.ShapeDtypeStruct((n_heads, rows, cols), F32),
        name=name,
    )(rel_bias, bucket)


def _bias_tables(rel_bias):
    kk = jnp.arange(3 * BLK, dtype=jnp.int32)[:, None]
    a = jnp.arange(BLK, dtype=jnp.int32)[None, :]
    rel = kk - BLK - a
    win_bucket = jnp.where(jnp.abs(rel) <= WINDOW, _t5_bucket(rel), -1)
    win = _expand_bias(rel_bias, win_bucket, 0, HA, "window_bias")
    r = jnp.arange(-STRIP_LEFT, STRIP_RIGHT + DIFF_TK, dtype=jnp.int32)[:, None]
    a = jnp.arange(DIFF_TQ, dtype=jnp.int32)[None, :]
    strip = _expand_bias(rel_bias, _t5_bucket(r - a), HA, HB, "diff_bias")
    return win, strip


def _layer(x, mod4, b_off, w_std, w_fm, w_out_a, w_out_b, ln_g2, ln_b2, sink2, win_bias, strip,
           subln_w2, lq1, lk1, lq2, lk2, lambda_init):
    ka, ga, kb, gb, qat, vat, qbt, vbt = _in_projection(x, mod4, b_off, w_std, w_fm)
    oa = _window_attention(qat, ka, vat, ga, win_bias, sink2)
    ob = _diff_attention(qbt, kb, vbt, gb, strip, subln_w2, lq1, lk1, lq2, lk2, lambda_init)
    return _out_projection(oa, ob, x, mod4, b_off, w_out_a, w_out_b, ln_g2, ln_b2)


def kernel(x_prompt, x_sample, c_prompt, c_sample, w_in, w_out, w_ada, b_ada, ln_g, ln_b, attn_sink,
           lambda_q1, lambda_k1, lambda_q2, lambda_k2, subln_w, rel_bias):
    assert w_in.shape[0] == DEPTH
    nbp, nbs = c_prompt.shape[0], c_sample.shape[0]
    pad = (-(nbp + nbs)) % 16
    c_all = jnp.concatenate([c_prompt, c_sample, jnp.zeros((pad, D_MODEL), F32)], axis=0)
    win_bias, strip = _bias_tables(rel_bias)
    xp, xs = x_prompt, x_sample
    for l in range(DEPTH):
        lambda_init = 0.8 - 0.6 * math.exp(-0.3 * l)
        w = w_in[l]
        sl = lambda n: w[:, _OFF[n][0]:_OFF[n][1]]
        w_std = jnp.concatenate([sl("ka"), sl("ga"), sl("kb"), sl("gb")], axis=1).astype(BF16)
        w_fm = jnp.concatenate([sl("qa"), sl("va"), sl("qb"), sl("vb")], axis=1).T.astype(BF16)
        w_out_a = w_out[l, :D_A].astype(BF16)
        w_out_b = w_out[l, D_A:].astype(BF16)
        mod = _modulation(c_all, w_ada[l], b_ada[l])
        mod4 = mod.reshape(mod.shape[0], 3, 1, D_MODEL)
        args = (w_std, w_fm, w_out_a, w_out_b, ln_g[l].reshape(1, -1), ln_b[l].reshape(1, -1),
                attn_sink[l] * LOG2E, win_bias, strip, subln_w[l].reshape(-1, 1),
                lambda_q1[l].reshape(1, -1), lambda_k1[l].reshape(1, -1),
                lambda_q2[l].reshape(1, -1), lambda_k2[l].reshape(1, -1), lambda_init)
        xp = _layer(xp, mod4, 0, *args)
        xs = _layer(xs, mod4, nbp, *args)
    return (xp, xs)
```

```python
import functools
import math

import jax
import jax.numpy as jnp
import numpy as np
from jax import lax
from jax.experimental import pallas as pl
from jax.experimental.pallas import tpu as pltpu

F32 = jnp.float32
BF16 = jnp.bfloat16

D_MODEL = 1024
HEAD_DIM = 64
HA = 8
HKV_A = 2
G_A = HA // HKV_A
HB = 4
E_B = 2 * HEAD_DIM
ONES_ROWS = 16
VB_ROWS = E_B + ONES_ROWS
VA_ROWS = HEAD_DIM + ONES_ROWS
D_A = HA * HEAD_DIM
D_KV_A = HKV_A * HEAD_DIM
D_B = HB * E_B
WINDOW = 128
BLK = 128
N_BUCKETS = 32
HALF_BUCKETS = N_BUCKETS // 2
MAX_EXACT = HALF_BUCKETS // 2
MAX_DIST = 128
DEPTH = 1
ALPHA = (2.0 * DEPTH) ** 0.25
LN_EPS = 1e-5
SUBLN_EPS = 1e-5
NEG_INF = -1e30
LOG2E = 1.4426950408889634
Q_SCALE = HEAD_DIM ** -0.5 * LOG2E

_OFF = {}
_o = 0
for _name, _w in (("qa", D_A), ("ka", D_KV_A), ("va", D_KV_A), ("ga", D_A),
                  ("qb", D_B), ("kb", D_B), ("vb", D_B), ("gb", D_B)):
    _OFF[_name] = (_o, _o + _w)
    _o += _w

ROW_TILE = 512
WIN_BLOCKS = 8
WIN_AHEAD = 2
DIFF_TQ = 256
DIFF_STEP_VISITS = 32
DIFF_TK = 512
DIFF_AHEAD = 2
DIFF_BUFFERS = DIFF_AHEAD + 2
STRIP_LEFT = DIFF_TK + MAX_DIST
STRIP_RIGHT = -(-(DIFF_TQ + MAX_DIST - 1) // 128) * 128
STRIP_ROWS = STRIP_LEFT + STRIP_RIGHT + DIFF_TK
VMEM_LIMIT = 56 * 1024 * 1024


def _t5_bucket(rel):
    assert (MAX_DIST // MAX_EXACT) ** 2 == 2 ** (HALF_BUCKETS - MAX_EXACT)
    rel = np.asarray(rel, np.int64)
    n = np.abs(rel)
    steps = sum((n * n >= MAX_EXACT ** 2 * 2 ** k).astype(np.int64)
                for k in range(1, HALF_BUCKETS - MAX_EXACT))
    large = np.minimum(MAX_EXACT + steps, HALF_BUCKETS - 1)
    return (np.where(rel > 0, HALF_BUCKETS, 0) + np.where(n < MAX_EXACT, n, large)).astype(np.int32)


def _silu(g):
    return g * jax.nn.sigmoid(g)


def _mod_kernel(c_ref, w_ref, b_ref, o_ref):
    a = _silu(c_ref[...])
    a_hi = a.astype(BF16)
    a_lo = (a - a_hi.astype(F32)).astype(BF16)
    w = w_ref[...]
    w_hi = w.astype(BF16)
    w_lo = (w - w_hi.astype(F32)).astype(BF16)
    acc = jnp.dot(a_hi, w_hi, preferred_element_type=F32)
    acc += jnp.dot(a_lo, w_hi, preferred_element_type=F32)
    acc += jnp.dot(a_hi, w_lo, preferred_element_type=F32)
    o_ref[...] = acc + b_ref[...]


def _modulation(c_all, w_ada, b_ada):
    rows = c_all.shape[0]
    n = w_ada.shape[1]
    tn = 512
    return pl.pallas_call(
        _mod_kernel,
        grid=(n // tn,),
        in_specs=[pl.BlockSpec((rows, D_MODEL), lambda j: (0, 0)),
                  pl.BlockSpec((D_MODEL, tn), lambda j: (0, j)),
                  pl.BlockSpec((1, tn), lambda j: (0, j))],
        out_specs=pl.BlockSpec((rows, tn), lambda j: (0, j)),
        out_shape=jax.ShapeDtypeStruct((rows, n), F32),
        name="modulation",
    )(c_all, w_ada, b_ada.reshape(1, n))


def _inproj_kernel(x_ref, shift_ref, scale_ref, wstd_ref, wfm_ref,
                   ka_ref, ga_ref, kb_ref, gb_ref, qat_ref, vat_ref, qbt_ref, vbt_ref):
    u = (x_ref[...] * (1.0 + scale_ref[...]) + shift_ref[...]).astype(BF16)
    r = jnp.dot(u, wstd_ref[...], preferred_element_type=F32)
    ka_ref[...] = r[:, 0:128].astype(BF16)
    ga_ref[...] = r[:, 128:640]
    kb_ref[...] = r[:, 640:1152].astype(BF16)
    gb_ref[...] = r[:, 1152:1664]
    rt = lax.dot_general(wfm_ref[...], u, (((1,), (1,)), ((), ())),
                         preferred_element_type=F32)
    qat_ref[...] = (rt[0:512] * Q_SCALE).astype(BF16)
    qbt_ref[...] = (rt[640:1152] * Q_SCALE).astype(BF16)
    ones = jnp.ones((ONES_ROWS, rt.shape[1]), BF16)
    for h in range(HKV_A):
        vat_ref[h, 0:HEAD_DIM, :] = rt[512 + h * HEAD_DIM:512 + (h + 1) * HEAD_DIM].astype(BF16)
        vat_ref[h, HEAD_DIM:VA_ROWS, :] = ones
    for h in range(HB):
        vbt_ref[h, 0:E_B, :] = rt[1152 + h * E_B:1152 + (h + 1) * E_B].astype(BF16)
        vbt_ref[h, E_B:VB_ROWS, :] = ones


def _in_projection(x, mod4, b_off, w_std, w_fm):
    bsz, seq, _ = x.shape
    tm = ROW_TILE
    n_std = w_std.shape[1]
    n_fm = w_fm.shape[0]
    row = lambda w: pl.BlockSpec((None, tm, w), lambda b, t: (b, t, 0))
    col = lambda h: pl.BlockSpec((None, h, tm), lambda b, t: (b, 0, t))
    vec = lambda k: pl.BlockSpec((None, None, 1, D_MODEL), lambda b, t: (b + b_off, k, 0, 0))
    return pl.pallas_call(
        _inproj_kernel,
        grid=(bsz, seq // tm),
        in_specs=[row(D_MODEL), vec(0), vec(1),
                  pl.BlockSpec((D_MODEL, n_std), lambda b, t: (0, 0)),
                  pl.BlockSpec((n_fm, D_MODEL), lambda b, t: (0, 0))],
        out_specs=[row(D_KV_A), row(D_A), row(D_B), row(D_B),
                   col(D_A),
                   pl.BlockSpec((None, HKV_A, VA_ROWS, tm), lambda b, t: (b, 0, 0, t)),
                   col(D_B),
                   pl.BlockSpec((None, HB, VB_ROWS, tm), lambda b, t: (b, 0, 0, t))],
        out_shape=[jax.ShapeDtypeStruct((bsz, seq, D_KV_A), BF16),
                   jax.ShapeDtypeStruct((bsz, seq, D_A), F32),
                   jax.ShapeDtypeStruct((bsz, seq, D_B), BF16),
                   jax.ShapeDtypeStruct((bsz, seq, D_B), F32),
                   jax.ShapeDtypeStruct((bsz, D_A, seq), BF16),
                   jax.ShapeDtypeStruct((bsz, HKV_A, VA_ROWS, seq), BF16),
                   jax.ShapeDtypeStruct((bsz, D_B, seq), BF16),
                   jax.ShapeDtypeStruct((bsz, HB, VB_ROWS, seq), BF16)],
        compiler_params=pltpu.CompilerParams(vmem_limit_bytes=VMEM_LIMIT),
        name="in_projection",
    )(x, mod4, mod4, w_std, w_fm)


def _win_kernel(zero_ref, sink_ref, qt_ref, kl_ref, kc_ref, kr_ref, vl_ref, vc_ref, vr_ref,
                bias_ref, g_ref, o_ref, *s_scs, n_steps, wb):
    i = pl.program_id(1)
    kall = jnp.concatenate([kl_ref[...], kc_ref[...], kr_ref[...]], axis=0)
    krow = lax.broadcasted_iota(jnp.int32, (3 * BLK, BLK), 0)
    off_l = jnp.where(krow < BLK, jnp.where(i == 0, NEG_INF, 0.0).astype(F32), 0.0)
    off_r = jnp.where(krow >= 2 * BLK, jnp.where(i == n_steps - 1, NEG_INF, 0.0).astype(F32), 0.0)
    zeros = jnp.zeros((HEAD_DIM, BLK), BF16)
    vall = [jnp.concatenate([vl_ref[hk], vc_ref[hk], vr_ref[hk]], axis=1) for hk in range(HKV_A)]
    units = [(blk, hk) for blk in range(wb) for hk in range(HKV_A)]

    def logits(u):
        blk, hk = units[u]
        blocks = []
        for g in range(G_A):
            h = hk * G_A + g
            q = qt_ref[h * HEAD_DIM:(h + 1) * HEAD_DIM, blk * BLK:(blk + 1) * BLK]
            blocks.append(jnp.concatenate([q, zeros] if hk == 0 else [zeros, q], axis=0))
        q_aug = jnp.concatenate(blocks, axis=1)
        return jnp.dot(kall[blk * BLK:(blk + 3) * BLK], q_aug, preferred_element_type=F32)

    def softmax_pv(u, s_sc):
        blk, hk = units[u]
        biases, sinks = [], []
        for g in range(G_A):
            h = hk * G_A + g
            bias = bias_ref[h]
            if blk == 0:
                bias = bias + off_l
            if blk == wb - 1:
                bias = bias + off_r
            biases.append(bias)
            sinks.append(jnp.full((1, BLK), sink_ref[h], F32))
        s_sc[rows_all, :] = s_sc[rows_all, :] + jnp.concatenate(biases, axis=1)
        sink = jnp.concatenate(sinks, axis=1)
        m = jnp.maximum(jnp.max(s_sc[rows_all, :], axis=0, keepdims=True), sink)
        pv = None
        for c0, cn in ((0, 2 * BLK), (2 * BLK, BLK)):
            rows = pl.ds(pl.multiple_of(zero_ref[0] + c0, 8), cn)
            p = jnp.exp2((s_sc[rows, :] - m).astype(BF16))
            part = jnp.dot(vall[hk][:, blk * BLK + c0:blk * BLK + c0 + cn], p,
                           preferred_element_type=F32)
            pv = part if pv is None else pv + part
        den = pv[HEAD_DIM:HEAD_DIM + 1] + jnp.exp2(sink - m)
        ot = pv[0:HEAD_DIM] * (1.0 / den)
        return [ot[:, g * BLK:(g + 1) * BLK] for g in range(G_A)]

    rows_all = pl.ds(pl.multiple_of(zero_ref[0], 8), 3 * BLK)
    nbuf = len(s_scs)
    for u in range(WIN_AHEAD):
        s_scs[u % nbuf][rows_all, :] = logits(u)
    outs = []
    for u, (blk, hk) in enumerate(units):
        if u + WIN_AHEAD < len(units):
            s_scs[(u + WIN_AHEAD) % nbuf][rows_all, :] = logits(u + WIN_AHEAD)
        outs += softmax_pv(u, s_scs[u % nbuf])
        if hk == HKV_A - 1:
            o = jnp.concatenate(outs, axis=0).T
            outs = []
            rows = slice(blk * BLK, (blk + 1) * BLK)
            o_ref[rows, :] = (o * _silu(g_ref[rows, :])).astype(BF16)


def _window_attention(qat, ka, vat, ga, bias_t, sink2):
    bsz, seq, _ = ka.shape
    nb = seq // BLK
    wb = WIN_BLOCKS
    n_steps = nb // wb
    kblk = lambda r, f: pl.BlockSpec((None, r, D_KV_A), lambda b, i: (b, f(i), 0))
    vblk = lambda r, f: pl.BlockSpec((None, HKV_A, VA_ROWS, r), lambda b, i: (b, 0, 0, f(i)))
    left = lambda i: jnp.maximum(i * wb - 1, 0)
    mid = lambda i: i
    right = lambda i: jnp.minimum((i + 1) * wb, nb - 1)
    return pl.pallas_call(
        functools.partial(_win_kernel, n_steps=n_steps, wb=wb),
        grid=(bsz, n_steps),
        in_specs=[pl.BlockSpec(memory_space=pltpu.SMEM),
                  pl.BlockSpec(memory_space=pltpu.SMEM),
                  pl.BlockSpec((None, D_A, wb * BLK), lambda b, i: (b, 0, i)),
                  kblk(BLK, left), kblk(wb * BLK, mid), kblk(BLK, right),
                  vblk(BLK, left), vblk(wb * BLK, mid), vblk(BLK, right),
                  pl.BlockSpec((HA, 3 * BLK, BLK), lambda b, i: (0, 0, 0)),
                  pl.BlockSpec((None, wb * BLK, D_A), lambda b, i: (b, i, 0))],
        out_specs=pl.BlockSpec((None, wb * BLK, D_A), lambda b, i: (b, i, 0)),
        out_shape=jax.ShapeDtypeStruct((bsz, seq, D_A), BF16),
        scratch_shapes=[pltpu.VMEM((3 * BLK + 8, G_A * BLK), F32)] * (WIN_AHEAD + 2),
        compiler_params=pltpu.CompilerParams(vmem_limit_bytes=VMEM_LIMIT),
        name="window_attention",
    )(jnp.zeros((1,), jnp.int32), sink2, qat, ka, ka, ka, vat, vat, vat, bias_t, ga)


def _diff_kernel(zero_ref, qt_ref, k_ref, vt_ref, strip_ref, g_ref, w_ref, lq1_ref, lk1_ref, lq2_ref,
                 lk2_ref, o_ref, *scratch, tq, tk, nk, n_sub, lambda_init):
    acc_scs, s_scs = scratch[:n_sub], scratch[n_sub:]
    st_rows = ld_rows = pl.ds(pl.multiple_of(zero_ref[0], 8), tk)
    c_left = jnp.concatenate([strip_ref[0:1, :]] * 2, axis=1)
    c_right = jnp.concatenate([strip_ref[STRIP_ROWS - 1:STRIP_ROWS, :]] * 2, axis=1)
    qrow = lax.broadcasted_iota(jnp.int32, (E_B, tq), 0)
    general_visits = (nk - 1, 0, 1)
    visit_order = general_visits + tuple(range(2, nk - 1))

    def q_tile(sub):
        qt = qt_ref[:, sub * tq:(sub + 1) * tq]
        zero = jnp.zeros_like(qt)
        q_aug = jnp.concatenate([jnp.where(qrow < HEAD_DIM, qt, zero),
                                 jnp.where(qrow >= HEAD_DIM, qt, zero)], axis=1)
        return pl.program_id(2) * n_sub + sub, q_aug

    def key_tile(i, jj):
        j = lax.div(i * tq, tk) + jj
        wrapped = j >= nk
        return jnp.where(wrapped, j - nk, j), wrapped

    def logits(i, q_aug, jj, s_sc):
        j, _ = key_tile(i, jj)
        k0 = pl.multiple_of(j * tk, tk)
        s_sc[st_rows, :] = jnp.dot(k_ref[pl.ds(k0, tk), :], q_aug, preferred_element_type=F32)

    def accumulate(i, jj, s_sc, acc_sc, m_prev):
        j, wrapped = key_tile(i, jj)
        k0 = pl.multiple_of(j * tk, tk)
        if jj in general_visits:
            r0 = jnp.clip(j * tk - i * tq, -STRIP_LEFT, STRIP_RIGHT) + STRIP_LEFT
            bias = strip_ref[pl.ds(pl.multiple_of(r0, 128), tk), :]
            s_sc[st_rows, :] = s_sc[ld_rows, :] + jnp.concatenate([bias, bias], axis=1)
            m_new = jnp.maximum(m_prev, jnp.max(s_sc[ld_rows, :], axis=0, keepdims=True))
            shift = m_new
        else:
            c = jnp.where(wrapped, c_left, c_right)
            m_new = jnp.maximum(m_prev, jnp.max(s_sc[ld_rows, :], axis=0, keepdims=True) + c)
            shift = m_new - c
        alpha = jnp.exp2(m_prev - m_new)
        pv = None
        for c0 in range(0, tk, 256):
            rows = pl.ds(pl.multiple_of(zero_ref[0] + c0, 8), 256)
            p = jnp.exp2((s_sc[rows, :] - shift).astype(BF16))
            part = jnp.dot(vt_ref[:, pl.ds(pl.multiple_of(k0 + c0, 256), 256)], p,
                           preferred_element_type=F32)
            pv = part if pv is None else pv + part
        if jj == visit_order[0]:
            acc_sc[...] = pv
        else:
            acc_sc[...] = acc_sc[...] * alpha + pv
        return m_new

    def finish(sub, acc_sc):
        lam = (jnp.exp(jnp.sum(lq1_ref[...] * lk1_ref[...], axis=1, keepdims=True))
               - jnp.exp(jnp.sum(lq2_ref[...] * lk2_ref[...], axis=1, keepdims=True)) + lambda_init)
        o_all = acc_sc[0:E_B, :] * (1.0 / acc_sc[E_B:E_B + 1, :])
        o = o_all[:, :tq] - lam * o_all[:, tq:]
        ms = jnp.mean(o * o, axis=0, keepdims=True)
        y = o * lax.rsqrt(ms + SUBLN_EPS) * w_ref[...] * (1.0 - lambda_init)
        rows = slice(sub * tq, (sub + 1) * tq)
        o_ref[rows, :] = (y.T * _silu(g_ref[rows, :])).astype(BF16)

    nbuf = len(s_scs)
    tiles = [q_tile(sub) for sub in range(n_sub)]
    visits = [(sub, jj) for sub in range(n_sub) for jj in visit_order]

    def issue_logits(v):
        sub, jj = visits[v]
        logits(*tiles[sub], jj, s_scs[v % nbuf])

    for v in range(DIFF_AHEAD):
        issue_logits(v)
    m = None
    for v, (sub, jj) in enumerate(visits):
        if v + DIFF_AHEAD < len(visits):
            issue_logits(v + DIFF_AHEAD)
        if jj == visit_order[0]:
            m = jnp.full((1, 2 * tq), NEG_INF, F32)
        m = accumulate(tiles[sub][0], jj, s_scs[v % nbuf], acc_scs[sub], m)
        if jj == visit_order[-1]:
            finish(sub, acc_scs[sub])


def _diff_attention(qbt, kb, vbt, gb, strip, subln_w2, lq1, lk1, lq2, lk2, lambda_init):
    bsz, seq, _ = kb.shape
    tq, tk = DIFF_TQ, DIFF_TK
    nk = seq // tk
    assert nk > DIFF_AHEAD and nk >= 3
    n_sub = max(1, DIFF_STEP_VISITS // nk)
    tq_step = tq * n_sub
    lam_spec = pl.BlockSpec((1, HEAD_DIM), lambda b, h, i: (0, 0))
    return pl.pallas_call(
        functools.partial(_diff_kernel, tq=tq, tk=tk, nk=nk, n_sub=n_sub, lambda_init=lambda_init),
        grid=(bsz, HB, seq // tq_step),
        in_specs=[pl.BlockSpec(memory_space=pltpu.SMEM),
                  pl.BlockSpec((None, E_B, tq_step), lambda b, h, i: (b, h, i)),
                  pl.BlockSpec((None, seq, E_B), lambda b, h, i: (b, 0, h)),
                  pl.BlockSpec((None, None, VB_ROWS, seq), lambda b, h, i: (b, h, 0, 0)),
                  pl.BlockSpec((None, STRIP_ROWS, tq), lambda b, h, i: (h, 0, 0)),
                  pl.BlockSpec((None, tq_step, E_B), lambda b, h, i: (b, i, h)),
                  pl.BlockSpec((E_B, 1), lambda b, h, i: (0, 0)),
                  lam_spec, lam_spec, lam_spec, lam_spec],
        out_specs=pl.BlockSpec((None, tq_step, E_B), lambda b, h, i: (b, i, h)),
        out_shape=jax.ShapeDtypeStruct((bsz, seq, D_B), BF16),
        scratch_shapes=[pltpu.VMEM((VB_ROWS, 2 * tq), F32)] * n_sub
                       + [pltpu.VMEM((tk + 8, 2 * tq), F32)] * DIFF_BUFFERS,
        compiler_params=pltpu.CompilerParams(vmem_limit_bytes=VMEM_LIMIT),
        name="diff_attention",
    )(jnp.zeros((1,), jnp.int32), qbt, kb, vbt, strip, gb, subln_w2, lq1, lk1, lq2, lk2)


def _outproj_kernel(oa_ref, ob_ref, x_ref, gate_ref, wa_ref, wb_ref, lng_ref, lnb_ref, y_ref):
    half = oa_ref.shape[0] // 2
    for r in range(2):
        rows = slice(r * half, (r + 1) * half)
        h = jnp.dot(oa_ref[rows, :], wa_ref[...], preferred_element_type=F32)
        h += jnp.dot(ob_ref[rows, :], wb_ref[...], preferred_element_type=F32)
        z = ALPHA * x_ref[rows, :] + gate_ref[...] * h
        mu = jnp.mean(z, axis=-1, keepdims=True)
        zc = z - mu
        var = jnp.mean(zc * zc, axis=-1, keepdims=True)
        y_ref[rows, :] = zc * lax.rsqrt(var + LN_EPS) * lng_ref[...] + lnb_ref[...]


def _out_projection(oa, ob, x, mod4, b_off, w_out_a, w_out_b, ln_g2, ln_b2):
    bsz, seq, _ = x.shape
    tm = ROW_TILE
    row = lambda w: pl.BlockSpec((None, tm, w), lambda b, t: (b, t, 0))
    full = lambda r, c: pl.BlockSpec((r, c), lambda b, t: (0, 0))
    return pl.pallas_call(
        _outproj_kernel,
        grid=(bsz, seq // tm),
        in_specs=[row(D_A), row(D_B), row(D_MODEL),
                  pl.BlockSpec((None, None, 1, D_MODEL), lambda b, t: (b + b_off, 2, 0, 0)),
                  full(D_A, D_MODEL), full(D_B, D_MODEL), full(1, D_MODEL), full(1, D_MODEL)],
        out_specs=row(D_MODEL),
        out_shape=jax.ShapeDtypeStruct((bsz, seq, D_MODEL), F32),
        compiler_params=pltpu.CompilerParams(vmem_limit_bytes=VMEM_LIMIT),
        name="out_projection",
    )(oa, ob, x, mod4, w_out_a, w_out_b, ln_g2, ln_b2)


def _bias_kernel(tab_ref, bucket_ref, o_ref, *, col0):
    h = pl.program_id(0) + col0
    bucket = bucket_ref[...]
    acc = jnp.full(bucket.shape, NEG_INF, F32)
    for b in range(N_BUCKETS):
        acc = jnp.where(bucket == b, tab_ref[b, h] * LOG2E, acc)
    o_ref[...] = acc


def _expand_bias(rel_bias, bucket, col0, n_heads, name):
    rows, cols = bucket.shape
    return pl.pallas_call(
        functools.partial(_bias_kernel, col0=col0),
        grid=(n_heads,),
        in_specs=[pl.BlockSpec(memory_space=pltpu.SMEM),
                  pl.BlockSpec((rows, cols), lambda h: (0, 0))],
        out_specs=pl.BlockSpec((None, rows, cols), lambda h: (h, 0, 0)),
        out_shape=jax.ShapeDtypeStruct((n_heads, rows, cols), F32),
        name=name,
    )(rel_bias, bucket)


def _bias_tables(rel_bias):
    kk = np.arange(3 * BLK)[:, None]
    a = np.arange(BLK)[None, :]
    rel = kk - BLK - a
    win_bucket = np.where(np.abs(rel) <= WINDOW, _t5_bucket(rel), -1).astype(np.int32)
    win = _expand_bias(rel_bias, jnp.asarray(win_bucket), 0, HA, "window_bias")
    r = np.arange(-STRIP_LEFT, STRIP_RIGHT + DIFF_TK)[:, None]
    a = np.arange(DIFF_TQ)[None, :]
    strip = _expand_bias(rel_bias, jnp.asarray(_t5_bucket(r - a)), HA, HB, "diff_bias")
    return win, strip


def _layer(x, mod4, b_off, w_std, w_fm, w_out_a, w_out_b, ln_g2, ln_b2, sink2, win_bias, strip,
           subln_w2, lq1, lk1, lq2, lk2, lambda_init):
    ka, ga, kb, gb, qat, vat, qbt, vbt = _in_projection(x, mod4, b_off, w_std, w_fm)
    oa = _window_attention(qat, ka, vat, ga, win_bias, sink2)
    ob = _diff_attention(qbt, kb, vbt, gb, strip, subln_w2, lq1, lk1, lq2, lk2, lambda_init)
    return _out_projection(oa, ob, x, mod4, b_off, w_out_a, w_out_b, ln_g2, ln_b2)


def kernel(x_prompt, x_sample, c_prompt, c_sample, w_in, w_out, w_ada, b_ada, ln_g, ln_b, attn_sink,
           lambda_q1, lambda_k1, lambda_q2, lambda_k2, subln_w, rel_bias):
    assert w_in.shape[0] == DEPTH
    nbp, nbs = c_prompt.shape[0], c_sample.shape[0]
    pad = (-(nbp + nbs)) % 16
    c_all = jnp.concatenate([c_prompt, c_sample, jnp.zeros((pad, D_MODEL), F32)], axis=0)
    win_bias, strip = _bias_tables(rel_bias)
    xp, xs = x_prompt, x_sample
    for l in range(DEPTH):
        lambda_init = 0.8 - 0.6 * math.exp(-0.3 * l)
        w = w_in[l]
        sl = lambda n: w[:, _OFF[n][0]:_OFF[n][1]]
        w_std = jnp.concatenate([sl("ka"), sl("ga"), sl("kb"), sl("gb")], axis=1).astype(BF16)
        w_fm = jnp.concatenate([sl("qa"), sl("va"), sl("qb"), sl("vb")], axis=1).T.astype(BF16)
        w_out_a = w_out[l, :D_A].astype(BF16)
        w_out_b = w_out[l, D_A:].astype(BF16)
        mod = _modulation(c_all, w_ada[l], b_ada[l])
        mod4 = mod.reshape(mod.shape[0], 3, 1, D_MODEL)
        args = (w_std, w_fm, w_out_a, w_out_b, ln_g[l].reshape(1, -1), ln_b[l].reshape(1, -1),
                attn_sink[l] * LOG2E, win_bias, strip, subln_w[l].reshape(-1, 1),
                lambda_q1[l].reshape(1, -1), lambda_k1[l].reshape(1, -1),
                lambda_q2[l].reshape(1, -1), lambda_k2[l].reshape(1, -1), lambda_init)
        xp = _layer(xp, mod4, 0, *args)
        xs = _layer(xs, mod4, nbp, *args)
    return (xp, xs)
```

```python
import functools
import math

import jax
import jax.numpy as jnp
import numpy as np
from jax import lax
from jax.experimental import pallas as pl
from jax.experimental.pallas import tpu as pltpu

F32 = jnp.float32
BF16 = jnp.bfloat16

D_MODEL = 1024
HEAD_DIM = 64
HA = 8
HKV_A = 2
G_A = HA // HKV_A
HB = 4
E_B = 2 * HEAD_DIM
ONES_ROWS = 16
VB_ROWS = E_B + ONES_ROWS
VA_ROWS = HEAD_DIM + ONES_ROWS
D_A = HA * HEAD_DIM
D_KV_A = HKV_A * HEAD_DIM
D_B = HB * E_B
WINDOW = 128
BLK = 128
N_BUCKETS = 32
HALF_BUCKETS = N_BUCKETS // 2
MAX_EXACT = HALF_BUCKETS // 2
MAX_DIST = 128
DEPTH = 1
ALPHA = (2.0 * DEPTH) ** 0.25
LN_EPS = 1e-5
SUBLN_EPS = 1e-5
NEG_INF = -1e30
LOG2E = 1.4426950408889634
Q_SCALE = HEAD_DIM ** -0.5 * LOG2E

_OFF = {}
_o = 0
for _name, _w in (("qa", D_A), ("ka", D_KV_A), ("va", D_KV_A), ("ga", D_A),
                  ("qb", D_B), ("kb", D_B), ("vb", D_B), ("gb", D_B)):
    _OFF[_name] = (_o, _o + _w)
    _o += _w

ROW_TILE = 512
OUT_ROW_TILE = 1024
OUT_CHUNK = 256
WIN_BLOCKS = 8
WIN_AHEAD = 3
DIFF_TQ = 256
DIFF_STEP_VISITS = 32
DIFF_TK = 512
DIFF_AHEAD = 2
DIFF_BUFFERS = DIFF_AHEAD + 2
STRIP_LEFT = DIFF_TK + MAX_DIST
STRIP_RIGHT = -(-(DIFF_TQ + MAX_DIST - 1) // 128) * 128
STRIP_ROWS = STRIP_LEFT + STRIP_RIGHT + DIFF_TK
VMEM_LIMIT = 56 * 1024 * 1024


def _t5_bucket(rel):
    assert (MAX_DIST // MAX_EXACT) ** 2 == 2 ** (HALF_BUCKETS - MAX_EXACT)
    rel = np.asarray(rel, np.int64)
    n = np.abs(rel)
    steps = sum((n * n >= MAX_EXACT ** 2 * 2 ** k).astype(np.int64)
                for k in range(1, HALF_BUCKETS - MAX_EXACT))
    large = np.minimum(MAX_EXACT + steps, HALF_BUCKETS - 1)
    return (np.where(rel > 0, HALF_BUCKETS, 0) + np.where(n < MAX_EXACT, n, large)).astype(np.int32)


def _silu(g):
    return g * jax.nn.sigmoid(g)


def _mod_kernel(c_ref, w_ref, b_ref, o_ref):
    a = _silu(c_ref[...])
    a_hi = a.astype(BF16)
    a_lo = (a - a_hi.astype(F32)).astype(BF16)
    w = w_ref[...]
    w_hi = w.astype(BF16)
    w_lo = (w - w_hi.astype(F32)).astype(BF16)
    acc = jnp.dot(a_hi, w_hi, preferred_element_type=F32)
    acc += jnp.dot(a_lo, w_hi, preferred_element_type=F32)
    acc += jnp.dot(a_hi, w_lo, preferred_element_type=F32)
    o_ref[...] = acc + b_ref[...]


def _modulation(c_all, w_ada, b_ada):
    rows = c_all.shape[0]
    n = w_ada.shape[1]
    tn = 512
    return pl.pallas_call(
        _mod_kernel,
        grid=(n // tn,),
        in_specs=[pl.BlockSpec((rows, D_MODEL), lambda j: (0, 0)),
                  pl.BlockSpec((D_MODEL, tn), lambda j: (0, j)),
                  pl.BlockSpec((1, tn), lambda j: (0, j))],
        out_specs=pl.BlockSpec((rows, tn), lambda j: (0, j)),
        out_shape=jax.ShapeDtypeStruct((rows, n), F32),
        name="modulation",
    )(c_all, w_ada, b_ada.reshape(1, n))


def _inproj_kernel(x_ref, shift_ref, scale_ref, wstd_ref, wfm_ref,
                   ka_ref, ga_ref, kb_ref, gb_ref, qat_ref, vat_ref, qbt_ref, vbt_ref):
    u = (x_ref[...] * (1.0 + scale_ref[...]) + shift_ref[...]).astype(BF16)
    r = jnp.dot(u, wstd_ref[...], preferred_element_type=F32)
    ka_ref[...] = r[:, 0:128].astype(BF16)
    ga_ref[...] = r[:, 128:640]
    kb_ref[...] = r[:, 640:1152].astype(BF16)
    gb_ref[...] = r[:, 1152:1664]
    rt = lax.dot_general(wfm_ref[...], u, (((1,), (1,)), ((), ())),
                         preferred_element_type=F32)
    qat_ref[...] = (rt[0:512] * Q_SCALE).astype(BF16)
    qbt_ref[...] = (rt[640:1152] * Q_SCALE).astype(BF16)
    ones = jnp.ones((ONES_ROWS, rt.shape[1]), BF16)
    for h in range(HKV_A):
        vat_ref[h, 0:HEAD_DIM, :] = rt[512 + h * HEAD_DIM:512 + (h + 1) * HEAD_DIM].astype(BF16)
        vat_ref[h, HEAD_DIM:VA_ROWS, :] = ones
    for h in range(HB):
        vbt_ref[h, 0:E_B, :] = rt[1152 + h * E_B:1152 + (h + 1) * E_B].astype(BF16)
        vbt_ref[h, E_B:VB_ROWS, :] = ones


def _in_projection(x, mod4, b_off, w_std, w_fm):
    bsz, seq, _ = x.shape
    tm = ROW_TILE
    n_std = w_std.shape[1]
    n_fm = w_fm.shape[0]
    row = lambda w: pl.BlockSpec((None, tm, w), lambda b, t: (b, t, 0))
    col = lambda h: pl.BlockSpec((None, h, tm), lambda b, t: (b, 0, t))
    vec = lambda k: pl.BlockSpec((None, None, 1, D_MODEL), lambda b, t: (b + b_off, k, 0, 0))
    return pl.pallas_call(
        _inproj_kernel,
        grid=(bsz, seq // tm),
        in_specs=[row(D_MODEL), vec(0), vec(1),
                  pl.BlockSpec((D_MODEL, n_std), lambda b, t: (0, 0)),
                  pl.BlockSpec((n_fm, D_MODEL), lambda b, t: (0, 0))],
        out_specs=[row(D_KV_A), row(D_A), row(D_B), row(D_B),
                   col(D_A),
                   pl.BlockSpec((None, HKV_A, VA_ROWS, tm), lambda b, t: (b, 0, 0, t)),
                   col(D_B),
                   pl.BlockSpec((None, HB, VB_ROWS, tm), lambda b, t: (b, 0, 0, t))],
        out_shape=[jax.ShapeDtypeStruct((bsz, seq, D_KV_A), BF16),
                   jax.ShapeDtypeStruct((bsz, seq, D_A), F32),
                   jax.ShapeDtypeStruct((bsz, seq, D_B), BF16),
                   jax.ShapeDtypeStruct((bsz, seq, D_B), F32),
                   jax.ShapeDtypeStruct((bsz, D_A, seq), BF16),
                   jax.ShapeDtypeStruct((bsz, HKV_A, VA_ROWS, seq), BF16),
                   jax.ShapeDtypeStruct((bsz, D_B, seq), BF16),
                   jax.ShapeDtypeStruct((bsz, HB, VB_ROWS, seq), BF16)],
        compiler_params=pltpu.CompilerParams(vmem_limit_bytes=VMEM_LIMIT),
        name="in_projection",
    )(x, mod4, mod4, w_std, w_fm)


def _win_kernel(zero_ref, sink_ref, qt_ref, kl_ref, kc_ref, kr_ref, vl_ref, vc_ref, vr_ref,
                bias_ref, g_ref, o_ref, *s_scs, n_steps, wb):
    i = pl.program_id(1)
    kall = jnp.concatenate([kl_ref[...], kc_ref[...], kr_ref[...]], axis=0)
    krow = lax.broadcasted_iota(jnp.int32, (3 * BLK, BLK), 0)
    off_l = jnp.where(krow < BLK, jnp.where(i == 0, NEG_INF, 0.0).astype(F32), 0.0)
    off_r = jnp.where(krow >= 2 * BLK, jnp.where(i == n_steps - 1, NEG_INF, 0.0).astype(F32), 0.0)
    zeros = jnp.zeros((HEAD_DIM, BLK), BF16)
    vall = [jnp.concatenate([vl_ref[hk], vc_ref[hk], vr_ref[hk]], axis=1) for hk in range(HKV_A)]
    units = [(blk, hk) for blk in range(wb) for hk in range(HKV_A)]

    def logits(u):
        blk, hk = units[u]
        blocks = []
        for g in range(G_A):
            h = hk * G_A + g
            q = qt_ref[h * HEAD_DIM:(h + 1) * HEAD_DIM, blk * BLK:(blk + 1) * BLK]
            blocks.append(jnp.concatenate([q, zeros] if hk == 0 else [zeros, q], axis=0))
        q_aug = jnp.concatenate(blocks, axis=1)
        return jnp.dot(kall[blk * BLK:(blk + 3) * BLK], q_aug, preferred_element_type=F32)

    def softmax_pv(u, s_sc):
        blk, hk = units[u]
        biases, sinks = [], []
        for g in range(G_A):
            h = hk * G_A + g
            bias = bias_ref[h]
            if blk == 0:
                bias = bias + off_l
            if blk == wb - 1:
                bias = bias + off_r
            biases.append(bias)
            sinks.append(jnp.full((1, BLK), sink_ref[h], F32))
        s_sc[rows_all, :] = s_sc[rows_all, :] + jnp.concatenate(biases, axis=1)
        sink = jnp.concatenate(sinks, axis=1)
        m = jnp.maximum(jnp.max(s_sc[rows_all, :], axis=0, keepdims=True), sink)
        pv = None
        for c0, cn in ((0, 2 * BLK), (2 * BLK, BLK)):
            rows = pl.ds(pl.multiple_of(zero_ref[0] + c0, 8), cn)
            p = jnp.exp2((s_sc[rows, :] - m).astype(BF16))
            part = jnp.dot(vall[hk][:, blk * BLK + c0:blk * BLK + c0 + cn], p,
                           preferred_element_type=F32)
            pv = part if pv is None else pv + part
        den = pv[HEAD_DIM:HEAD_DIM + 1] + jnp.exp2(sink - m)
        ot = pv[0:HEAD_DIM] * (1.0 / den)
        return [ot[:, g * BLK:(g + 1) * BLK] for g in range(G_A)]

    rows_all = pl.ds(pl.multiple_of(zero_ref[0], 8), 3 * BLK)
    nbuf = len(s_scs)
    for u in range(WIN_AHEAD):
        s_scs[u % nbuf][rows_all, :] = logits(u)
    outs = []
    for u, (blk, hk) in enumerate(units):
        if u + WIN_AHEAD < len(units):
            s_scs[(u + WIN_AHEAD) % nbuf][rows_all, :] = logits(u + WIN_AHEAD)
        outs += softmax_pv(u, s_scs[u % nbuf])
        if hk == HKV_A - 1:
            o = jnp.concatenate(outs, axis=0).T
            outs = []
            rows = slice(blk * BLK, (blk + 1) * BLK)
            o_ref[rows, :] = (o * _silu(g_ref[rows, :])).astype(BF16)


def _window_attention(qat, ka, vat, ga, bias_t, sink2):
    bsz, seq, _ = ka.shape
    nb = seq // BLK
    wb = WIN_BLOCKS
    n_steps = nb // wb
    kblk = lambda r, f: pl.BlockSpec((None, r, D_KV_A), lambda b, i: (b, f(i), 0))
    vblk = lambda r, f: pl.BlockSpec((None, HKV_A, VA_ROWS, r), lambda b, i: (b, 0, 0, f(i)))
    left = lambda i: jnp.maximum(i * wb - 1, 0)
    mid = lambda i: i
    right = lambda i: jnp.minimum((i + 1) * wb, nb - 1)
    return pl.pallas_call(
        functools.partial(_win_kernel, n_steps=n_steps, wb=wb),
        grid=(bsz, n_steps),
        in_specs=[pl.BlockSpec(memory_space=pltpu.SMEM),
                  pl.BlockSpec(memory_space=pltpu.SMEM),
                  pl.BlockSpec((None, D_A, wb * BLK), lambda b, i: (b, 0, i)),
                  kblk(BLK, left), kblk(wb * BLK, mid), kblk(BLK, right),
                  vblk(BLK, left), vblk(wb * BLK, mid), vblk(BLK, right),
                  pl.BlockSpec((HA, 3 * BLK, BLK), lambda b, i: (0, 0, 0)),
                  pl.BlockSpec((None, wb * BLK, D_A), lambda b, i: (b, i, 0))],
        out_specs=pl.BlockSpec((None, wb * BLK, D_A), lambda b, i: (b, i, 0)),
        out_shape=jax.ShapeDtypeStruct((bsz, seq, D_A), BF16),
        scratch_shapes=[pltpu.VMEM((3 * BLK + 8, G_A * BLK), F32)] * (WIN_AHEAD + 2),
        compiler_params=pltpu.CompilerParams(vmem_limit_bytes=VMEM_LIMIT),
        name="window_attention",
    )(jnp.zeros((1,), jnp.int32), sink2, qat, ka, ka, ka, vat, vat, vat, bias_t, ga)


def _diff_kernel(zero_ref, qt_ref, k_ref, vt_ref, strip_ref, g_ref, w_ref, lq1_ref, lk1_ref, lq2_ref,
                 lk2_ref, o_ref, *scratch, tq, tk, nk, n_sub, lambda_init):
    acc_scs, s_scs = scratch[:n_sub], scratch[n_sub:]
    st_rows = ld_rows = pl.ds(pl.multiple_of(zero_ref[0], 8), tk)
    c_left = jnp.concatenate([strip_ref[0:1, :]] * 2, axis=1)
    c_right = jnp.concatenate([strip_ref[STRIP_ROWS - 1:STRIP_ROWS, :]] * 2, axis=1)
    qrow = lax.broadcasted_iota(jnp.int32, (E_B, tq), 0)
    general_visits = (nk - 1, 0, 1)
    visit_order = general_visits + tuple(range(2, nk - 1))

    def q_tile(sub):
        qt = qt_ref[:, sub * tq:(sub + 1) * tq]
        zero = jnp.zeros_like(qt)
        q_aug = jnp.concatenate([jnp.where(qrow < HEAD_DIM, qt, zero),
                                 jnp.where(qrow >= HEAD_DIM, qt, zero)], axis=1)
        return pl.program_id(2) * n_sub + sub, q_aug

    def key_tile(i, jj):
        j = lax.div(i * tq, tk) + jj
        wrapped = j >= nk
        return jnp.where(wrapped, j - nk, j), wrapped

    def logits(i, q_aug, jj, s_sc):
        j, _ = key_tile(i, jj)
        k0 = pl.multiple_of(j * tk, tk)
        s_sc[st_rows, :] = jnp.dot(k_ref[pl.ds(k0, tk), :], q_aug, preferred_element_type=F32)

    def accumulate(i, jj, s_sc, acc_sc, m_prev):
        j, wrapped = key_tile(i, jj)
        k0 = pl.multiple_of(j * tk, tk)
        if jj in general_visits:
            r0 = jnp.clip(j * tk - i * tq, -STRIP_LEFT, STRIP_RIGHT) + STRIP_LEFT
            bias = strip_ref[pl.ds(pl.multiple_of(r0, 128), tk), :]
            s_sc[st_rows, :] = s_sc[ld_rows, :] + jnp.concatenate([bias, bias], axis=1)
            m_new = jnp.maximum(m_prev, jnp.max(s_sc[ld_rows, :], axis=0, keepdims=True))
            shift = m_new
        else:
            c = jnp.where(wrapped, c_left, c_right)
            m_new = jnp.maximum(m_prev, jnp.max(s_sc[ld_rows, :], axis=0, keepdims=True) + c)
            shift = m_new - c
        alpha = jnp.exp2(m_prev - m_new)
        pv = None
        for c0 in range(0, tk, 256):
            rows = pl.ds(pl.multiple_of(zero_ref[0] + c0, 8), 256)
            p = jnp.exp2((s_sc[rows, :] - shift).astype(BF16))
            part = jnp.dot(vt_ref[:, pl.ds(pl.multiple_of(k0 + c0, 256), 256)], p,
                           preferred_element_type=F32)
            pv = part if pv is None else pv + part
        if jj == visit_order[0]:
            acc_sc[...] = pv
        else:
            acc_sc[...] = acc_sc[...] * alpha + pv
        return m_new

    def finish(sub, acc_sc):
        lam = (jnp.exp(jnp.sum(lq1_ref[...] * lk1_ref[...], axis=1, keepdims=True))
               - jnp.exp(jnp.sum(lq2_ref[...] * lk2_ref[...], axis=1, keepdims=True)) + lambda_init)
        o_all = acc_sc[0:E_B, :] * (1.0 / acc_sc[E_B:E_B + 1, :])
        o = o_all[:, :tq] - lam * o_all[:, tq:]
        ms = jnp.mean(o * o, axis=0, keepdims=True)
        y = o * lax.rsqrt(ms + SUBLN_EPS) * w_ref[...] * (1.0 - lambda_init)
        rows = slice(sub * tq, (sub + 1) * tq)
        o_ref[rows, :] = (y.T * _silu(g_ref[rows, :])).astype(BF16)

    nbuf = len(s_scs)
    tiles = [q_tile(sub) for sub in range(n_sub)]
    visits = [(sub, jj) for sub in range(n_sub) for jj in visit_order]

    def issue_logits(v):
        sub, jj = visits[v]
        logits(*tiles[sub], jj, s_scs[v % nbuf])

    for v in range(DIFF_AHEAD):
        issue_logits(v)
    m = None
    for v, (sub, jj) in enumerate(visits):
        if v + DIFF_AHEAD < len(visits):
            issue_logits(v + DIFF_AHEAD)
        if jj == visit_order[0]:
            m = jnp.full((1, 2 * tq), NEG_INF, F32)
        m = accumulate(tiles[sub][0], jj, s_scs[v % nbuf], acc_scs[sub], m)
        if jj == visit_order[-1]:
            finish(sub, acc_scs[sub])


def _diff_attention(qbt, kb, vbt, gb, strip, subln_w2, lq1, lk1, lq2, lk2, lambda_init):
    bsz, seq, _ = kb.shape
    tq, tk = DIFF_TQ, DIFF_TK
    nk = seq // tk
    assert nk > DIFF_AHEAD and nk >= 3
    n_sub = max(1, DIFF_STEP_VISITS // nk)
    tq_step = tq * n_sub
    lam_spec = pl.BlockSpec((1, HEAD_DIM), lambda b, h, i: (0, 0))
    return pl.pallas_call(
        functools.partial(_diff_kernel, tq=tq, tk=tk, nk=nk, n_sub=n_sub, lambda_init=lambda_init),
        grid=(bsz, HB, seq // tq_step),
        in_specs=[pl.BlockSpec(memory_space=pltpu.SMEM),
                  pl.BlockSpec((None, E_B, tq_step), lambda b, h, i: (b, h, i)),
                  pl.BlockSpec((None, seq, E_B), lambda b, h, i: (b, 0, h)),
                  pl.BlockSpec((None, None, VB_ROWS, seq), lambda b, h, i: (b, h, 0, 0)),
                  pl.BlockSpec((None, STRIP_ROWS, tq), lambda b, h, i: (h, 0, 0)),
                  pl.BlockSpec((None, tq_step, E_B), lambda b, h, i: (b, i, h)),
                  pl.BlockSpec((E_B, 1), lambda b, h, i: (0, 0)),
                  lam_spec, lam_spec, lam_spec, lam_spec],
        out_specs=pl.BlockSpec((None, tq_step, E_B), lambda b, h, i: (b, i, h)),
        out_shape=jax.ShapeDtypeStruct((bsz, seq, D_B), BF16),
        scratch_shapes=[pltpu.VMEM((VB_ROWS, 2 * tq), F32)] * n_sub
                       + [pltpu.VMEM((tk + 8, 2 * tq), F32)] * DIFF_BUFFERS,
        compiler_params=pltpu.CompilerParams(vmem_limit_bytes=VMEM_LIMIT),
        name="diff_attention",
    )(jnp.zeros((1,), jnp.int32), qbt, kb, vbt, strip, gb, subln_w2, lq1, lk1, lq2, lk2)


def _outproj_kernel(oa_ref, ob_ref, x_ref, gate_ref, wa_ref, wb_ref, lng_ref, lnb_ref, y_ref):
    for r in range(oa_ref.shape[0] // OUT_CHUNK):
        rows = slice(r * OUT_CHUNK, (r + 1) * OUT_CHUNK)
        h = jnp.dot(oa_ref[rows, :], wa_ref[...], preferred_element_type=F32)
        h += jnp.dot(ob_ref[rows, :], wb_ref[...], preferred_element_type=F32)
        z = ALPHA * x_ref[rows, :] + gate_ref[...] * h
        mu = jnp.mean(z, axis=-1, keepdims=True)
        zc = z - mu
        var = jnp.mean(zc * zc, axis=-1, keepdims=True)
        y_ref[rows, :] = zc * lax.rsqrt(var + LN_EPS) * lng_ref[...] + lnb_ref[...]


def _out_projection(oa, ob, x, mod4, b_off, w_out_a, w_out_b, ln_g2, ln_b2):
    bsz, seq, _ = x.shape
    tm = OUT_ROW_TILE
    row = lambda w: pl.BlockSpec((None, tm, w), lambda b, t: (b, t, 0))
    full = lambda r, c: pl.BlockSpec((r, c), lambda b, t: (0, 0))
    return pl.pallas_call(
        _outproj_kernel,
        grid=(bsz, seq // tm),
        in_specs=[row(D_A), row(D_B), row(D_MODEL),
                  pl.BlockSpec((None, None, 1, D_MODEL), lambda b, t: (b + b_off, 2, 0, 0)),
                  full(D_A, D_MODEL), full(D_B, D_MODEL), full(1, D_MODEL), full(1, D_MODEL)],
        out_specs=row(D_MODEL),
        out_shape=jax.ShapeDtypeStruct((bsz, seq, D_MODEL), F32),
        compiler_params=pltpu.CompilerParams(vmem_limit_bytes=VMEM_LIMIT),
        name="out_projection",
    )(oa, ob, x, mod4, w_out_a, w_out_b, ln_g2, ln_b2)


def _bias_kernel(tab_ref, bucket_ref, o_ref, *, col0):
    h = pl.program_id(0) + col0
    bucket = bucket_ref[...]
    acc = jnp.full(bucket.shape, NEG_INF, F32)
    for b in range(N_BUCKETS):
        acc = jnp.where(bucket == b, tab_ref[b, h] * LOG2E, acc)
    o_ref[...] = acc


def _expand_bias(rel_bias, bucket, col0, n_heads, name):
    rows, cols = bucket.shape
    return pl.pallas_call(
        functools.partial(_bias_kernel, col0=col0),
        grid=(n_heads,),
        in_specs=[pl.BlockSpec(memory_space=pltpu.SMEM),
                  pl.BlockSpec((rows, cols), lambda h: (0, 0))],
        out_specs=pl.BlockSpec((None, rows, cols), lambda h: (h, 0, 0)),
        out_shape=jax.ShapeDtypeStruct((n_heads, rows, cols), F32),
        name=name,
    )(rel_bias, bucket)


def _bias_tables(rel_bias):
    kk = np.arange(3 * BLK)[:, None]
    a = np.arange(BLK)[None, :]
    rel = kk - BLK - a
    win_bucket = np.where(np.abs(rel) <= WINDOW, _t5_bucket(rel), -1).astype(np.int32)
    win = _expand_bias(rel_bias, jnp.asarray(win_bucket), 0, HA, "window_bias")
    r = np.arange(-STRIP_LEFT, STRIP_RIGHT + DIFF_TK)[:, None]
    a = np.arange(DIFF_TQ)[None, :]
    strip = _expand_bias(rel_bias, jnp.asarray(_t5_bucket(r - a)), HA, HB, "diff_bias")
    return win, strip


def _layer(x, mod4, b_off, w_std, w_fm, w_out_a, w_out_b, ln_g2, ln_b2, sink2, win_bias, strip,
           subln_w2, lq1, lk1, lq2, lk2, lambda_init):
    ka, ga, kb, gb, qat, vat, qbt, vbt = _in_projection(x, mod4, b_off, w_std, w_fm)
    oa = _window_attention(qat, ka, vat, ga, win_bias, sink2)
    ob = _diff_attention(qbt, kb, vbt, gb, strip, subln_w2, lq1, lk1, lq2, lk2, lambda_init)
    return _out_projection(oa, ob, x, mod4, b_off, w_out_a, w_out_b, ln_g2, ln_b2)


def kernel(x_prompt, x_sample, c_prompt, c_sample, w_in, w_out, w_ada, b_ada, ln_g, ln_b, attn_sink,
           lambda_q1, lambda_k1, lambda_q2, lambda_k2, subln_w, rel_bias):
    assert w_in.shape[0] == DEPTH
    nbp, nbs = c_prompt.shape[0], c_sample.shape[0]
    pad = (-(nbp + nbs)) % 16
    c_all = jnp.concatenate([c_prompt, c_sample, jnp.zeros((pad, D_MODEL), F32)], axis=0)
    win_bias, strip = _bias_tables(rel_bias)
    xp, xs = x_prompt, x_sample
    for l in range(DEPTH):
        lambda_init = 0.8 - 0.6 * math.exp(-0.3 * l)
        w = w_in[l]
        sl = lambda n: w[:, _OFF[n][0]:_OFF[n][1]]
        w_std = jnp.concatenate([sl("ka"), sl("ga"), sl("kb"), sl("gb")], axis=1).astype(BF16)
        w_fm = jnp.concatenate([sl("qa"), sl("va"), sl("qb"), sl("vb")], axis=1).T.astype(BF16)
        w_out_a = w_out[l, :D_A].astype(BF16)
        w_out_b = w_out[l, D_A:].astype(BF16)
        mod = _modulation(c_all, w_ada[l], b_ada[l])
        mod4 = mod.reshape(mod.shape[0], 3, 1, D_MODEL)
        args = (w_std, w_fm, w_out_a, w_out_b, ln_g[l].reshape(1, -1), ln_b[l].reshape(1, -1),
                attn_sink[l] * LOG2E, win_bias, strip, subln_w[l].reshape(-1, 1),
                lambda_q1[l].reshape(1, -1), lambda_k1[l].reshape(1, -1),
                lambda_q2[l].reshape(1, -1), lambda_k2[l].reshape(1, -1), lambda_init)
        xp = _layer(xp, mod4, 0, *args)
        xs = _layer(xs, mod4, nbp, *args)
    return (xp, xs)
```

```python
import functools
import math

import jax
import jax.numpy as jnp
import numpy as np
from jax import lax
from jax.experimental import pallas as pl
from jax.experimental.pallas import tpu as pltpu

F32 = jnp.float32
BF16 = jnp.bfloat16

D_MODEL = 1024
HEAD_DIM = 64
HA = 8
HKV_A = 2
G_A = HA // HKV_A
HB = 4
E_B = 2 * HEAD_DIM
ONES_ROWS = 16
VB_ROWS = E_B + ONES_ROWS
VA_ROWS = HEAD_DIM + ONES_ROWS
D_A = HA * HEAD_DIM
D_KV_A = HKV_A * HEAD_DIM
D_B = HB * E_B
WINDOW = 128
BLK = 128
N_BUCKETS = 32
HALF_BUCKETS = N_BUCKETS // 2
MAX_EXACT = HALF_BUCKETS // 2
MAX_DIST = 128
DEPTH = 1
ALPHA = (2.0 * DEPTH) ** 0.25
LN_EPS = 1e-5
SUBLN_EPS = 1e-5
NEG_INF = -1e30
LOG2E = 1.4426950408889634
Q_SCALE = HEAD_DIM ** -0.5 * LOG2E

_OFF = {}
_o = 0
for _name, _w in (("qa", D_A), ("ka", D_KV_A), ("va", D_KV_A), ("ga", D_A),
                  ("qb", D_B), ("kb", D_B), ("vb", D_B), ("gb", D_B)):
    _OFF[_name] = (_o, _o + _w)
    _o += _w

ROW_TILE = 512
OUT_ROW_TILE = 2048
OUT_CHUNK = 256
WIN_BLOCKS = 8
WIN_AHEAD = 3
DIFF_TQ = 256
DIFF_STEP_VISITS = 64
DIFF_TK = 512
DIFF_AHEAD = 2
DIFF_BUFFERS = DIFF_AHEAD + 2
STRIP_LEFT = DIFF_TK + MAX_DIST
STRIP_RIGHT = -(-(DIFF_TQ + MAX_DIST - 1) // 128) * 128
STRIP_ROWS = STRIP_LEFT + STRIP_RIGHT + DIFF_TK
VMEM_LIMIT = 56 * 1024 * 1024


def _t5_bucket(rel):
    assert (MAX_DIST // MAX_EXACT) ** 2 == 2 ** (HALF_BUCKETS - MAX_EXACT)
    rel = np.asarray(rel, np.int64)
    n = np.abs(rel)
    steps = sum((n * n >= MAX_EXACT ** 2 * 2 ** k).astype(np.int64)
                for k in range(1, HALF_BUCKETS - MAX_EXACT))
    large = np.minimum(MAX_EXACT + steps, HALF_BUCKETS - 1)
    return (np.where(rel > 0, HALF_BUCKETS, 0) + np.where(n < MAX_EXACT, n, large)).astype(np.int32)


def _silu(g):
    return g * jax.nn.sigmoid(g)


def _mod_kernel(c_ref, w_ref, b_ref, o_ref):
    a = _silu(c_ref[...])
    a_hi = a.astype(BF16)
    a_lo = (a - a_hi.astype(F32)).astype(BF16)
    w = w_ref[...]
    w_hi = w.astype(BF16)
    w_lo = (w - w_hi.astype(F32)).astype(BF16)
    acc = jnp.dot(a_hi, w_hi, preferred_element_type=F32)
    acc += jnp.dot(a_lo, w_hi, preferred_element_type=F32)
    acc += jnp.dot(a_hi, w_lo, preferred_element_type=F32)
    o_ref[...] = acc + b_ref[...]


def _modulation(c_all, w_ada, b_ada):
    rows = c_all.shape[0]
    n = w_ada.shape[1]
    tn = 512
    return pl.pallas_call(
        _mod_kernel,
        grid=(n // tn,),
        in_specs=[pl.BlockSpec((rows, D_MODEL), lambda j: (0, 0)),
                  pl.BlockSpec((D_MODEL, tn), lambda j: (0, j)),
                  pl.BlockSpec((1, tn), lambda j: (0, j))],
        out_specs=pl.BlockSpec((rows, tn), lambda j: (0, j)),
        out_shape=jax.ShapeDtypeStruct((rows, n), F32),
        name="modulation",
    )(c_all, w_ada, b_ada.reshape(1, n))


def _inproj_kernel(x_ref, shift_ref, scale_ref, wstd_ref, wfm_ref,
                   ka_ref, ga_ref, kb_ref, gb_ref, qat_ref, vat_ref, qbt_ref, vbt_ref):
    u = (x_ref[...] * (1.0 + scale_ref[...]) + shift_ref[...]).astype(BF16)
    r = jnp.dot(u, wstd_ref[...], preferred_element_type=F32)
    ka_ref[...] = r[:, 0:128].astype(BF16)
    ga_ref[...] = r[:, 128:640]
    kb_ref[...] = r[:, 640:1152].astype(BF16)
    gb_ref[...] = r[:, 1152:1664]
    rt = lax.dot_general(wfm_ref[...], u, (((1,), (1,)), ((), ())),
                         preferred_element_type=F32)
    qat_ref[...] = (rt[0:512] * Q_SCALE).astype(BF16)
    qbt_ref[...] = (rt[640:1152] * Q_SCALE).astype(BF16)
    ones = jnp.ones((ONES_ROWS, rt.shape[1]), BF16)
    for h in range(HKV_A):
        vat_ref[h, 0:HEAD_DIM, :] = rt[512 + h * HEAD_DIM:512 + (h + 1) * HEAD_DIM].astype(BF16)
        vat_ref[h, HEAD_DIM:VA_ROWS, :] = ones
    for h in range(HB):
        vbt_ref[h, 0:E_B, :] = rt[1152 + h * E_B:1152 + (h + 1) * E_B].astype(BF16)
        vbt_ref[h, E_B:VB_ROWS, :] = ones


def _in_projection(x, mod4, b_off, w_std, w_fm):
    bsz, seq, _ = x.shape
    tm = ROW_TILE
    n_std = w_std.shape[1]
    n_fm = w_fm.shape[0]
    row = lambda w: pl.BlockSpec((None, tm, w), lambda b, t: (b, t, 0))
    col = lambda h: pl.BlockSpec((None, h, tm), lambda b, t: (b, 0, t))
    vec = lambda k: pl.BlockSpec((None, None, 1, D_MODEL), lambda b, t: (b + b_off, k, 0, 0))
    return pl.pallas_call(
        _inproj_kernel,
        grid=(bsz, seq // tm),
        in_specs=[row(D_MODEL), vec(0), vec(1),
                  pl.BlockSpec((D_MODEL, n_std), lambda b, t: (0, 0)),
                  pl.BlockSpec((n_fm, D_MODEL), lambda b, t: (0, 0))],
        out_specs=[row(D_KV_A), row(D_A), row(D_B), row(D_B),
                   col(D_A),
                   pl.BlockSpec((None, HKV_A, VA_ROWS, tm), lambda b, t: (b, 0, 0, t)),
                   col(D_B),
                   pl.BlockSpec((None, HB, VB_ROWS, tm), lambda b, t: (b, 0, 0, t))],
        out_shape=[jax.ShapeDtypeStruct((bsz, seq, D_KV_A), BF16),
                   jax.ShapeDtypeStruct((bsz, seq, D_A), F32),
                   jax.ShapeDtypeStruct((bsz, seq, D_B), BF16),
                   jax.ShapeDtypeStruct((bsz, seq, D_B), F32),
                   jax.ShapeDtypeStruct((bsz, D_A, seq), BF16),
                   jax.ShapeDtypeStruct((bsz, HKV_A, VA_ROWS, seq), BF16),
                   jax.ShapeDtypeStruct((bsz, D_B, seq), BF16),
                   jax.ShapeDtypeStruct((bsz, HB, VB_ROWS, seq), BF16)],
        compiler_params=pltpu.CompilerParams(vmem_limit_bytes=VMEM_LIMIT),
        name="in_projection",
    )(x, mod4, mod4, w_std, w_fm)


def _win_kernel(zero_ref, sink_ref, qt_ref, kl_ref, kc_ref, kr_ref, vl_ref, vc_ref, vr_ref,
                bias_ref, g_ref, o_ref, *s_scs, n_steps, wb):
    i = pl.program_id(1)
    kall = jnp.concatenate([kl_ref[...], kc_ref[...], kr_ref[...]], axis=0)
    krow = lax.broadcasted_iota(jnp.int32, (3 * BLK, BLK), 0)
    off_l = jnp.where(krow < BLK, jnp.where(i == 0, NEG_INF, 0.0).astype(F32), 0.0)
    off_r = jnp.where(krow >= 2 * BLK, jnp.where(i == n_steps - 1, NEG_INF, 0.0).astype(F32), 0.0)
    zeros = jnp.zeros((HEAD_DIM, BLK), BF16)
    vall = [jnp.concatenate([vl_ref[hk], vc_ref[hk], vr_ref[hk]], axis=1) for hk in range(HKV_A)]
    units = [(blk, hk) for blk in range(wb) for hk in range(HKV_A)]

    def logits(u):
        blk, hk = units[u]
        blocks = []
        for g in range(G_A):
            h = hk * G_A + g
            q = qt_ref[h * HEAD_DIM:(h + 1) * HEAD_DIM, blk * BLK:(blk + 1) * BLK]
            blocks.append(jnp.concatenate([q, zeros] if hk == 0 else [zeros, q], axis=0))
        q_aug = jnp.concatenate(blocks, axis=1)
        return jnp.dot(kall[blk * BLK:(blk + 3) * BLK], q_aug, preferred_element_type=F32)

    def softmax_pv(u, s_sc):
        blk, hk = units[u]
        biases, sinks = [], []
        for g in range(G_A):
            h = hk * G_A + g
            bias = bias_ref[h]
            if blk == 0:
                bias = bias + off_l
            if blk == wb - 1:
                bias = bias + off_r
            biases.append(bias)
            sinks.append(jnp.full((1, BLK), sink_ref[h], F32))
        s_sc[rows_all, :] = s_sc[rows_all, :] + jnp.concatenate(biases, axis=1)
        sink = jnp.concatenate(sinks, axis=1)
        m = jnp.maximum(jnp.max(s_sc[rows_all, :], axis=0, keepdims=True), sink)
        pv = None
        for c0, cn in ((0, 2 * BLK), (2 * BLK, BLK)):
            rows = pl.ds(pl.multiple_of(zero_ref[0] + c0, 8), cn)
            p = jnp.exp2((s_sc[rows, :] - m).astype(BF16))
            part = jnp.dot(vall[hk][:, blk * BLK + c0:blk * BLK + c0 + cn], p,
                           preferred_element_type=F32)
            pv = part if pv is None else pv + part
        den = pv[HEAD_DIM:HEAD_DIM + 1] + jnp.exp2(sink - m)
        ot = pv[0:HEAD_DIM] * (1.0 / den)
        return [ot[:, g * BLK:(g + 1) * BLK] for g in range(G_A)]

    rows_all = pl.ds(pl.multiple_of(zero_ref[0], 8), 3 * BLK)
    nbuf = len(s_scs)
    for u in range(WIN_AHEAD):
        s_scs[u % nbuf][rows_all, :] = logits(u)
    outs = []
    for u, (blk, hk) in enumerate(units):
        if u + WIN_AHEAD < len(units):
            s_scs[(u + WIN_AHEAD) % nbuf][rows_all, :] = logits(u + WIN_AHEAD)
        outs += softmax_pv(u, s_scs[u % nbuf])
        if hk == HKV_A - 1:
            o = jnp.concatenate(outs, axis=0).T
            outs = []
            rows = slice(blk * BLK, (blk + 1) * BLK)
            o_ref[rows, :] = (o * _silu(g_ref[rows, :])).astype(BF16)


def _window_attention(qat, ka, vat, ga, bias_t, sink2):
    bsz, seq, _ = ka.shape
    nb = seq // BLK
    wb = WIN_BLOCKS
    n_steps = nb // wb
    kblk = lambda r, f: pl.BlockSpec((None, r, D_KV_A), lambda b, i: (b, f(i), 0))
    vblk = lambda r, f: pl.BlockSpec((None, HKV_A, VA_ROWS, r), lambda b, i: (b, 0, 0, f(i)))
    left = lambda i: jnp.maximum(i * wb - 1, 0)
    mid = lambda i: i
    right = lambda i: jnp.minimum((i + 1) * wb, nb - 1)
    return pl.pallas_call(
        functools.partial(_win_kernel, n_steps=n_steps, wb=wb),
        grid=(bsz, n_steps),
        in_specs=[pl.BlockSpec(memory_space=pltpu.SMEM),
                  pl.BlockSpec(memory_space=pltpu.SMEM),
                  pl.BlockSpec((None, D_A, wb * BLK), lambda b, i: (b, 0, i)),
                  kblk(BLK, left), kblk(wb * BLK, mid), kblk(BLK, right),
                  vblk(BLK, left), vblk(wb * BLK, mid), vblk(BLK, right),
                  pl.BlockSpec((HA, 3 * BLK, BLK), lambda b, i: (0, 0, 0)),
                  pl.BlockSpec((None, wb * BLK, D_A), lambda b, i: (b, i, 0))],
        out_specs=pl.BlockSpec((None, wb * BLK, D_A), lambda b, i: (b, i, 0)),
        out_shape=jax.ShapeDtypeStruct((bsz, seq, D_A), BF16),
        scratch_shapes=[pltpu.VMEM((3 * BLK + 8, G_A * BLK), F32)] * (WIN_AHEAD + 2),
        compiler_params=pltpu.CompilerParams(vmem_limit_bytes=VMEM_LIMIT),
        name="window_attention",
    )(jnp.zeros((1,), jnp.int32), sink2, qat, ka, ka, ka, vat, vat, vat, bias_t, ga)


def _diff_kernel(zero_ref, qt_ref, k_ref, vt_ref, strip_ref, g_ref, w_ref, lq1_ref, lk1_ref, lq2_ref,
                 lk2_ref, o_ref, *scratch, tq, tk, nk, n_sub, lambda_init):
    acc_scs, s_scs = scratch[:n_sub], scratch[n_sub:]
    st_rows = ld_rows = pl.ds(pl.multiple_of(zero_ref[0], 8), tk)
    c_left = jnp.concatenate([strip_ref[0:1, :]] * 2, axis=1)
    c_right = jnp.concatenate([strip_ref[STRIP_ROWS - 1:STRIP_ROWS, :]] * 2, axis=1)
    qrow = lax.broadcasted_iota(jnp.int32, (E_B, tq), 0)
    general_visits = (nk - 1, 0, 1)
    visit_order = general_visits + tuple(range(2, nk - 1))

    def q_tile(sub):
        qt = qt_ref[:, sub * tq:(sub + 1) * tq]
        zero = jnp.zeros_like(qt)
        q_aug = jnp.concatenate([jnp.where(qrow < HEAD_DIM, qt, zero),
                                 jnp.where(qrow >= HEAD_DIM, qt, zero)], axis=1)
        return pl.program_id(2) * n_sub + sub, q_aug

    def key_tile(i, jj):
        j = lax.div(i * tq, tk) + jj
        wrapped = j >= nk
        return jnp.where(wrapped, j - nk, j), wrapped

    def logits(i, q_aug, jj, s_sc):
        j, _ = key_tile(i, jj)
        k0 = pl.multiple_of(j * tk, tk)
        s_sc[st_rows, :] = jnp.dot(k_ref[pl.ds(k0, tk), :], q_aug, preferred_element_type=F32)

    def accumulate(i, jj, s_sc, acc_sc, m_prev):
        j, wrapped = key_tile(i, jj)
        k0 = pl.multiple_of(j * tk, tk)
        if jj in general_visits:
            r0 = jnp.clip(j * tk - i * tq, -STRIP_LEFT, STRIP_RIGHT) + STRIP_LEFT
            bias = strip_ref[pl.ds(pl.multiple_of(r0, 128), tk), :]
            s_sc[st_rows, :] = s_sc[ld_rows, :] + jnp.concatenate([bias, bias], axis=1)
            m_new = jnp.maximum(m_prev, jnp.max(s_sc[ld_rows, :], axis=0, keepdims=True))
            shift = m_new
        else:
            c = jnp.where(wrapped, c_left, c_right)
            m_new = jnp.maximum(m_prev, jnp.max(s_sc[ld_rows, :], axis=0, keepdims=True) + c)
            shift = m_new - c
        alpha = jnp.exp2(m_prev - m_new)
        pv = None
        for c0 in range(0, tk, 256):
            rows = pl.ds(pl.multiple_of(zero_ref[0] + c0, 8), 256)
            p = jnp.exp2((s_sc[rows, :] - shift).astype(BF16))
            part = jnp.dot(vt_ref[:, pl.ds(pl.multiple_of(k0 + c0, 256), 256)], p,
                           preferred_element_type=F32)
            pv = part if pv is None else pv + part
        if jj == visit_order[0]:
            acc_sc[...] = pv
        else:
            acc_sc[...] = acc_sc[...] * alpha + pv
        return m_new

    def finish(sub, acc_sc):
        lam = (jnp.exp(jnp.sum(lq1_ref[...] * lk1_ref[...], axis=1, keepdims=True))
               - jnp.exp(jnp.sum(lq2_ref[...] * lk2_ref[...], axis=1, keepdims=True)) + lambda_init)
        o_all = acc_sc[0:E_B, :] * (1.0 / acc_sc[E_B:E_B + 1, :])
        o = o_all[:, :tq] - lam * o_all[:, tq:]
        ms = jnp.mean(o * o, axis=0, keepdims=True)
        y = o * lax.rsqrt(ms + SUBLN_EPS) * w_ref[...] * (1.0 - lambda_init)
        rows = slice(sub * tq, (sub + 1) * tq)
        o_ref[rows, :] = (y.T * _silu(g_ref[rows, :])).astype(BF16)

    nbuf = len(s_scs)
    tiles = [q_tile(sub) for sub in range(n_sub)]
    visits = [(sub, jj) for sub in range(n_sub) for jj in visit_order]

    def issue_logits(v):
        sub, jj = visits[v]
        logits(*tiles[sub], jj, s_scs[v % nbuf])

    for v in range(DIFF_AHEAD):
        issue_logits(v)
    m = None
    for v, (sub, jj) in enumerate(visits):
        if v + DIFF_AHEAD < len(visits):
            issue_logits(v + DIFF_AHEAD)
        if jj == visit_order[0]:
            m = jnp.full((1, 2 * tq), NEG_INF, F32)
        m = accumulate(tiles[sub][0], jj, s_scs[v % nbuf], acc_scs[sub], m)
        if jj == visit_order[-1]:
            finish(sub, acc_scs[sub])


def _diff_attention(qbt, kb, vbt, gb, strip, subln_w2, lq1, lk1, lq2, lk2, lambda_init):
    bsz, seq, _ = kb.shape
    tq, tk = DIFF_TQ, DIFF_TK
    nk = seq // tk
    assert nk > DIFF_AHEAD and nk >= 3
    n_sub = max(1, DIFF_STEP_VISITS // nk)
    tq_step = tq * n_sub
    lam_spec = pl.BlockSpec((1, HEAD_DIM), lambda b, h, i: (0, 0))
    return pl.pallas_call(
        functools.partial(_diff_kernel, tq=tq, tk=tk, nk=nk, n_sub=n_sub, lambda_init=lambda_init),
        grid=(bsz, HB, seq // tq_step),
        in_specs=[pl.BlockSpec(memory_space=pltpu.SMEM),
                  pl.BlockSpec((None, E_B, tq_step), lambda b, h, i: (b, h, i)),
                  pl.BlockSpec((None, seq, E_B), lambda b, h, i: (b, 0, h)),
                  pl.BlockSpec((None, None, VB_ROWS, seq), lambda b, h, i: (b, h, 0, 0)),
                  pl.BlockSpec((None, STRIP_ROWS, tq), lambda b, h, i: (h, 0, 0)),
                  pl.BlockSpec((None, tq_step, E_B), lambda b, h, i: (b, i, h)),
                  pl.BlockSpec((E_B, 1), lambda b, h, i: (0, 0)),
                  lam_spec, lam_spec, lam_spec, lam_spec],
        out_specs=pl.BlockSpec((None, tq_step, E_B), lambda b, h, i: (b, i, h)),
        out_shape=jax.ShapeDtypeStruct((bsz, seq, D_B), BF16),
        scratch_shapes=[pltpu.VMEM((VB_ROWS, 2 * tq), F32)] * n_sub
                       + [pltpu.VMEM((tk + 8, 2 * tq), F32)] * DIFF_BUFFERS,
        compiler_params=pltpu.CompilerParams(vmem_limit_bytes=VMEM_LIMIT),
        name="diff_attention",
    )(jnp.zeros((1,), jnp.int32), qbt, kb, vbt, strip, gb, subln_w2, lq1, lk1, lq2, lk2)


def _outproj_kernel(oa_ref, ob_ref, x_ref, gate_ref, wa_ref, wb_ref, lng_ref, lnb_ref, y_ref):
    for r in range(oa_ref.shape[0] // OUT_CHUNK):
        rows = slice(r * OUT_CHUNK, (r + 1) * OUT_CHUNK)
        h = jnp.dot(oa_ref[rows, :], wa_ref[...], preferred_element_type=F32)
        h += jnp.dot(ob_ref[rows, :], wb_ref[...], preferred_element_type=F32)
        z = ALPHA * x_ref[rows, :] + gate_ref[...] * h
        mu = jnp.mean(z, axis=-1, keepdims=True)
        zc = z - mu
        var = jnp.mean(zc * zc, axis=-1, keepdims=True)
        y_ref[rows, :] = zc * lax.rsqrt(var + LN_EPS) * lng_ref[...] + lnb_ref[...]


def _out_projection(oa, ob, x, mod4, b_off, w_out_a, w_out_b, ln_g2, ln_b2):
    bsz, seq, _ = x.shape
    tm = OUT_ROW_TILE
    row = lambda w: pl.BlockSpec((None, tm, w), lambda b, t: (b, t, 0))
    full = lambda r, c: pl.BlockSpec((r, c), lambda b, t: (0, 0))
    return pl.pallas_call(
        _outproj_kernel,
        grid=(bsz, seq // tm),
        in_specs=[row(D_A), row(D_B), row(D_MODEL),
                  pl.BlockSpec((None, None, 1, D_MODEL), lambda b, t: (b + b_off, 2, 0, 0)),
                  full(D_A, D_MODEL), full(D_B, D_MODEL), full(1, D_MODEL), full(1, D_MODEL)],
        out_specs=row(D_MODEL),
        out_shape=jax.ShapeDtypeStruct((bsz, seq, D_MODEL), F32),
        compiler_params=pltpu.CompilerParams(vmem_limit_bytes=VMEM_LIMIT),
        name="out_projection",
    )(oa, ob, x, mod4, w_out_a, w_out_b, ln_g2, ln_b2)


def _bias_kernel(tab_ref, bucket_ref, o_ref, *, col0):
    h = pl.program_id(0) + col0
    bucket = bucket_ref[...]
    acc = jnp.full(bucket.shape, NEG_INF, F32)
    for b in range(N_BUCKETS):
        acc = jnp.where(bucket == b, tab_ref[b, h] * LOG2E, acc)
    o_ref[...] = acc


def _expand_bias(rel_bias, bucket, col0, n_heads, name):
    rows, cols = bucket.shape
    return pl.pallas_call(
        functools.partial(_bias_kernel, col0=col0),
        grid=(n_heads,),
        in_specs=[pl.BlockSpec(memory_space=pltpu.SMEM),
                  pl.BlockSpec((rows, cols), lambda h: (0, 0))],
        out_specs=pl.BlockSpec((None, rows, cols), lambda h: (h, 0, 0)),
        out_shape=jax.ShapeDtypeStruct((n_heads, rows, cols), F32),
        name=name,
    )(rel_bias, bucket)


def _bias_tables(rel_bias):
    kk = np.arange(3 * BLK)[:, None]
    a = np.arange(BLK)[None, :]
    rel = kk - BLK - a
    win_bucket = np.where(np.abs(rel) <= WINDOW, _t5_bucket(rel), -1).astype(np.int32)
    win = _expand_bias(rel_bias, jnp.asarray(win_bucket), 0, HA, "window_bias")
    r = np.arange(-STRIP_LEFT, STRIP_RIGHT + DIFF_TK)[:, None]
    a = np.arange(DIFF_TQ)[None, :]
    strip = _expand_bias(rel_bias, jnp.asarray(_t5_bucket(r - a)), HA, HB, "diff_bias")
    return win, strip


def _layer(x, mod4, b_off, w_std, w_fm, w_out_a, w_out_b, ln_g2, ln_b2, sink2, win_bias, strip,
           subln_w2, lq1, lk1, lq2, lk2, lambda_init):
    ka, ga, kb, gb, qat, vat, qbt, vbt = _in_projection(x, mod4, b_off, w_std, w_fm)
    oa = _window_attention(qat, ka, vat, ga, win_bias, sink2)
    ob = _diff_attention(qbt, kb, vbt, gb, strip, subln_w2, lq1, lk1, lq2, lk2, lambda_init)
    return _out_projection(oa, ob, x, mod4, b_off, w_out_a, w_out_b, ln_g2, ln_b2)


def kernel(x_prompt, x_sample, c_prompt, c_sample, w_in, w_out, w_ada, b_ada, ln_g, ln_b, attn_sink,
           lambda_q1, lambda_k1, lambda_q2, lambda_k2, subln_w, rel_bias):
    assert w_in.shape[0] == DEPTH
    nbp, nbs = c_prompt.shape[0], c_sample.shape[0]
    pad = (-(nbp + nbs)) % 16
    c_all = jnp.concatenate([c_prompt, c_sample, jnp.zeros((pad, D_MODEL), F32)], axis=0)
    win_bias, strip = _bias_tables(rel_bias)
    xp, xs = x_prompt, x_sample
    for l in range(DEPTH):
        lambda_init = 0.8 - 0.6 * math.exp(-0.3 * l)
        w = w_in[l]
        sl = lambda n: w[:, _OFF[n][0]:_OFF[n][1]]
        w_std = jnp.concatenate([sl("ka"), sl("ga"), sl("kb"), sl("gb")], axis=1).astype(BF16)
        w_fm = jnp.concatenate([sl("qa"), sl("va"), sl("qb"), sl("vb")], axis=1).T.astype(BF16)
        w_out_a = w_out[l, :D_A].astype(BF16)
        w_out_b = w_out[l, D_A:].astype(BF16)
        mod = _modulation(c_all, w_ada[l], b_ada[l])
        mod4 = mod.reshape(mod.shape[0], 3, 1, D_MODEL)
        args = (w_std, w_fm, w_out_a, w_out_b, ln_g[l].reshape(1, -1), ln_b[l].reshape(1, -1),
                attn_sink[l] * LOG2E, win_bias, strip, subln_w[l].reshape(-1, 1),
                lambda_q1[l].reshape(1, -1), lambda_k1[l].reshape(1, -1),
                lambda_q2[l].reshape(1, -1), lambda_k2[l].reshape(1, -1), lambda_init)
        xp = _layer(xp, mod4, 0, *args)
        xs = _layer(xs, mod4, nbp, *args)
    return (xp, xs)
```

```python
import functools
import math

import jax
import jax.numpy as jnp
import numpy as np
from jax import lax
from jax.experimental import pallas as pl
from jax.experimental.pallas import tpu as pltpu

F32 = jnp.float32
BF16 = jnp.bfloat16

D_MODEL = 1024
HEAD_DIM = 64
HA = 8
HKV_A = 2
G_A = HA // HKV_A
HB = 4
E_B = 2 * HEAD_DIM
ONES_ROWS = 16
VB_ROWS = E_B + ONES_ROWS
VA_ROWS = HEAD_DIM + ONES_ROWS
D_A = HA * HEAD_DIM
D_KV_A = HKV_A * HEAD_DIM
D_B = HB * E_B
WINDOW = 128
BLK = 128
N_BUCKETS = 32
HALF_BUCKETS = N_BUCKETS // 2
MAX_EXACT = HALF_BUCKETS // 2
MAX_DIST = 128
DEPTH = 1
ALPHA = (2.0 * DEPTH) ** 0.25
LN_EPS = 1e-5
SUBLN_EPS = 1e-5
NEG_INF = -1e30
LOG2E = 1.4426950408889634
Q_SCALE = HEAD_DIM ** -0.5 * LOG2E

_OFF = {}
_o = 0
for _name, _w in (("qa", D_A), ("ka", D_KV_A), ("va", D_KV_A), ("ga", D_A),
                  ("qb", D_B), ("kb", D_B), ("vb", D_B), ("gb", D_B)):
    _OFF[_name] = (_o, _o + _w)
    _o += _w

ROW_TILE = 512
OUT_ROW_TILE = 2048
OUT_CHUNK = 256
WIN_BLOCKS = 8
WIN_AHEAD = 3
DIFF_TQ = 256
DIFF_SUBTILES = 4
DIFF_TK = 512
DIFF_AHEAD = 2
DIFF_BUFFERS = DIFF_AHEAD + 2
STRIP_LEFT = DIFF_TK + MAX_DIST
STRIP_RIGHT = -(-(DIFF_TQ + MAX_DIST - 1) // 128) * 128
STRIP_ROWS = STRIP_LEFT + STRIP_RIGHT + DIFF_TK
VMEM_LIMIT = 56 * 1024 * 1024


def _t5_bucket(rel):
    assert (MAX_DIST // MAX_EXACT) ** 2 == 2 ** (HALF_BUCKETS - MAX_EXACT)
    rel = np.asarray(rel, np.int64)
    n = np.abs(rel)
    steps = sum((n * n >= MAX_EXACT ** 2 * 2 ** k).astype(np.int64)
                for k in range(1, HALF_BUCKETS - MAX_EXACT))
    large = np.minimum(MAX_EXACT + steps, HALF_BUCKETS - 1)
    return (np.where(rel > 0, HALF_BUCKETS, 0) + np.where(n < MAX_EXACT, n, large)).astype(np.int32)


def _silu(g):
    return g * jax.nn.sigmoid(g)


def _mod_kernel(c_ref, w_ref, b_ref, o_ref):
    a = _silu(c_ref[...])
    a_hi = a.astype(BF16)
    a_lo = (a - a_hi.astype(F32)).astype(BF16)
    w = w_ref[...]
    w_hi = w.astype(BF16)
    w_lo = (w - w_hi.astype(F32)).astype(BF16)
    acc = jnp.dot(a_hi, w_hi, preferred_element_type=F32)
    acc += jnp.dot(a_lo, w_hi, preferred_element_type=F32)
    acc += jnp.dot(a_hi, w_lo, preferred_element_type=F32)
    o_ref[...] = acc + b_ref[...]


def _modulation(c_all, w_ada, b_ada):
    rows = c_all.shape[0]
    n = w_ada.shape[1]
    tn = 512
    return pl.pallas_call(
        _mod_kernel,
        grid=(n // tn,),
        in_specs=[pl.BlockSpec((rows, D_MODEL), lambda j: (0, 0)),
                  pl.BlockSpec((D_MODEL, tn), lambda j: (0, j)),
                  pl.BlockSpec((1, tn), lambda j: (0, j))],
        out_specs=pl.BlockSpec((rows, tn), lambda j: (0, j)),
        out_shape=jax.ShapeDtypeStruct((rows, n), F32),
        name="modulation",
    )(c_all, w_ada, b_ada.reshape(1, n))


def _inproj_kernel(x_ref, shift_ref, scale_ref, wstd_ref, wfm_ref,
                   ka_ref, ga_ref, kb_ref, gb_ref, qat_ref, vat_ref, qbt_ref, vbt_ref):
    u = (x_ref[...] * (1.0 + scale_ref[...]) + shift_ref[...]).astype(BF16)
    r = jnp.dot(u, wstd_ref[...], preferred_element_type=F32)
    ka_ref[...] = r[:, 0:128].astype(BF16)
    ga_ref[...] = r[:, 128:640]
    kb_ref[...] = r[:, 640:1152].astype(BF16)
    gb_ref[...] = r[:, 1152:1664]
    rt = lax.dot_general(wfm_ref[...], u, (((1,), (1,)), ((), ())),
                         preferred_element_type=F32)
    qat_ref[...] = (rt[0:512] * Q_SCALE).astype(BF16)
    qbt_ref[...] = (rt[640:1152] * Q_SCALE).astype(BF16)
    ones = jnp.ones((ONES_ROWS, rt.shape[1]), BF16)
    for h in range(HKV_A):
        vat_ref[h, 0:HEAD_DIM, :] = rt[512 + h * HEAD_DIM:512 + (h + 1) * HEAD_DIM].astype(BF16)
        vat_ref[h, HEAD_DIM:VA_ROWS, :] = ones
    for h in range(HB):
        vbt_ref[h, 0:E_B, :] = rt[1152 + h * E_B:1152 + (h + 1) * E_B].astype(BF16)
        vbt_ref[h, E_B:VB_ROWS, :] = ones


def _in_projection(x, mod4, b_off, w_std, w_fm):
    bsz, seq, _ = x.shape
    tm = ROW_TILE
    n_std = w_std.shape[1]
    n_fm = w_fm.shape[0]
    row = lambda w: pl.BlockSpec((None, tm, w), lambda b, t: (b, t, 0))
    col = lambda h: pl.BlockSpec((None, h, tm), lambda b, t: (b, 0, t))
    vec = lambda k: pl.BlockSpec((None, None, 1, D_MODEL), lambda b, t: (b + b_off, k, 0, 0))
    return pl.pallas_call(
        _inproj_kernel,
        grid=(bsz, seq // tm),
        in_specs=[row(D_MODEL), vec(0), vec(1),
                  pl.BlockSpec((D_MODEL, n_std), lambda b, t: (0, 0)),
                  pl.BlockSpec((n_fm, D_MODEL), lambda b, t: (0, 0))],
        out_specs=[row(D_KV_A), row(D_A), row(D_B), row(D_B),
                   col(D_A),
                   pl.BlockSpec((None, HKV_A, VA_ROWS, tm), lambda b, t: (b, 0, 0, t)),
                   col(D_B),
                   pl.BlockSpec((None, HB, VB_ROWS, tm), lambda b, t: (b, 0, 0, t))],
        out_shape=[jax.ShapeDtypeStruct((bsz, seq, D_KV_A), BF16),
                   jax.ShapeDtypeStruct((bsz, seq, D_A), F32),
                   jax.ShapeDtypeStruct((bsz, seq, D_B), BF16),
                   jax.ShapeDtypeStruct((bsz, seq, D_B), F32),
                   jax.ShapeDtypeStruct((bsz, D_A, seq), BF16),
                   jax.ShapeDtypeStruct((bsz, HKV_A, VA_ROWS, seq), BF16),
                   jax.ShapeDtypeStruct((bsz, D_B, seq), BF16),
                   jax.ShapeDtypeStruct((bsz, HB, VB_ROWS, seq), BF16)],
        compiler_params=pltpu.CompilerParams(vmem_limit_bytes=VMEM_LIMIT),
        name="in_projection",
    )(x, mod4, mod4, w_std, w_fm)


def _win_kernel(zero_ref, sink_ref, qt_ref, kl_ref, kc_ref, kr_ref, vl_ref, vc_ref, vr_ref,
                bias_ref, g_ref, o_ref, *s_scs, n_steps, wb):
    i = pl.program_id(1)
    kall = jnp.concatenate([kl_ref[...], kc_ref[...], kr_ref[...]], axis=0)
    krow = lax.broadcasted_iota(jnp.int32, (3 * BLK, BLK), 0)
    off_l = jnp.where(krow < BLK, jnp.where(i == 0, NEG_INF, 0.0).astype(F32), 0.0)
    off_r = jnp.where(krow >= 2 * BLK, jnp.where(i == n_steps - 1, NEG_INF, 0.0).astype(F32), 0.0)
    zeros = jnp.zeros((HEAD_DIM, BLK), BF16)
    vall = [jnp.concatenate([vl_ref[hk], vc_ref[hk], vr_ref[hk]], axis=1) for hk in range(HKV_A)]
    units = [(blk, hk) for blk in range(wb) for hk in range(HKV_A)]

    def logits(u):
        blk, hk = units[u]
        blocks = []
        for g in range(G_A):
            h = hk * G_A + g
            q = qt_ref[h * HEAD_DIM:(h + 1) * HEAD_DIM, blk * BLK:(blk + 1) * BLK]
            blocks.append(jnp.concatenate([q, zeros] if hk == 0 else [zeros, q], axis=0))
        q_aug = jnp.concatenate(blocks, axis=1)
        return jnp.dot(kall[blk * BLK:(blk + 3) * BLK], q_aug, preferred_element_type=F32)

    def softmax_pv(u, s_sc):
        blk, hk = units[u]
        biases, sinks = [], []
        for g in range(G_A):
            h = hk * G_A + g
            bias = bias_ref[h]
            if blk == 0:
                bias = bias + off_l
            if blk == wb - 1:
                bias = bias + off_r
            biases.append(bias)
            sinks.append(jnp.full((1, BLK), sink_ref[h], F32))
        s_sc[rows_all, :] = s_sc[rows_all, :] + jnp.concatenate(biases, axis=1)
        sink = jnp.concatenate(sinks, axis=1)
        m = jnp.maximum(jnp.max(s_sc[rows_all, :], axis=0, keepdims=True), sink)
        pv = None
        for c0, cn in ((0, 2 * BLK), (2 * BLK, BLK)):
            rows = pl.ds(pl.multiple_of(zero_ref[0] + c0, 8), cn)
            p = jnp.exp2((s_sc[rows, :] - m).astype(BF16))
            part = jnp.dot(vall[hk][:, blk * BLK + c0:blk * BLK + c0 + cn], p,
                           preferred_element_type=F32)
            pv = part if pv is None else pv + part
        den = pv[HEAD_DIM:HEAD_DIM + 1] + jnp.exp2(sink - m)
        ot = pv[0:HEAD_DIM] * (1.0 / den)
        return [ot[:, g * BLK:(g + 1) * BLK] for g in range(G_A)]

    rows_all = pl.ds(pl.multiple_of(zero_ref[0], 8), 3 * BLK)
    nbuf = len(s_scs)
    for u in range(WIN_AHEAD):
        s_scs[u % nbuf][rows_all, :] = logits(u)
    outs = []
    for u, (blk, hk) in enumerate(units):
        if u + WIN_AHEAD < len(units):
            s_scs[(u + WIN_AHEAD) % nbuf][rows_all, :] = logits(u + WIN_AHEAD)
        outs += softmax_pv(u, s_scs[u % nbuf])
        if hk == HKV_A - 1:
            o = jnp.concatenate(outs, axis=0).T
            outs = []
            rows = slice(blk * BLK, (blk + 1) * BLK)
            o_ref[rows, :] = (o * _silu(g_ref[rows, :])).astype(BF16)


def _window_attention(qat, ka, vat, ga, bias_t, sink2):
    bsz, seq, _ = ka.shape
    nb = seq // BLK
    wb = WIN_BLOCKS
    n_steps = nb // wb
    kblk = lambda r, f: pl.BlockSpec((None, r, D_KV_A), lambda b, i: (b, f(i), 0))
    vblk = lambda r, f: pl.BlockSpec((None, HKV_A, VA_ROWS, r), lambda b, i: (b, 0, 0, f(i)))
    left = lambda i: jnp.maximum(i * wb - 1, 0)
    mid = lambda i: i
    right = lambda i: jnp.minimum((i + 1) * wb, nb - 1)
    return pl.pallas_call(
        functools.partial(_win_kernel, n_steps=n_steps, wb=wb),
        grid=(bsz, n_steps),
        in_specs=[pl.BlockSpec(memory_space=pltpu.SMEM),
                  pl.BlockSpec(memory_space=pltpu.SMEM),
                  pl.BlockSpec((None, D_A, wb * BLK), lambda b, i: (b, 0, i)),
                  kblk(BLK, left), kblk(wb * BLK, mid), kblk(BLK, right),
                  vblk(BLK, left), vblk(wb * BLK, mid), vblk(BLK, right),
                  pl.BlockSpec((HA, 3 * BLK, BLK), lambda b, i: (0, 0, 0)),
                  pl.BlockSpec((None, wb * BLK, D_A), lambda b, i: (b, i, 0))],
        out_specs=pl.BlockSpec((None, wb * BLK, D_A), lambda b, i: (b, i, 0)),
        out_shape=jax.ShapeDtypeStruct((bsz, seq, D_A), BF16),
        scratch_shapes=[pltpu.VMEM((3 * BLK + 8, G_A * BLK), F32)] * (WIN_AHEAD + 2),
        compiler_params=pltpu.CompilerParams(vmem_limit_bytes=VMEM_LIMIT),
        name="window_attention",
    )(jnp.zeros((1,), jnp.int32), sink2, qat, ka, ka, ka, vat, vat, vat, bias_t, ga)


def _diff_kernel(zero_ref, qt_ref, k_ref, vt_ref, strip_ref, g_ref, w_ref, lq1_ref, lk1_ref, lq2_ref,
                 lk2_ref, o_ref, *scratch, tq, tk, nk, n_sub, lambda_init):
    acc_scs, s_scs = scratch[:n_sub], scratch[n_sub:]
    st_rows = ld_rows = pl.ds(pl.multiple_of(zero_ref[0], 8), tk)
    c_left = jnp.concatenate([strip_ref[0:1, :]] * 2, axis=1)
    c_right = jnp.concatenate([strip_ref[STRIP_ROWS - 1:STRIP_ROWS, :]] * 2, axis=1)
    qrow = lax.broadcasted_iota(jnp.int32, (E_B, tq), 0)
    general_visits = (nk - 1, 0, 1)
    visit_order = general_visits + tuple(range(2, nk - 1))

    def q_tile(sub):
        qt = qt_ref[:, sub * tq:(sub + 1) * tq]
        zero = jnp.zeros_like(qt)
        q_aug = jnp.concatenate([jnp.where(qrow < HEAD_DIM, qt, zero),
                                 jnp.where(qrow >= HEAD_DIM, qt, zero)], axis=1)
        return pl.program_id(2) * n_sub + sub, q_aug

    def key_tile(i, jj):
        j = lax.div(i * tq, tk) + jj
        wrapped = j >= nk
        return jnp.where(wrapped, j - nk, j), wrapped

    def logits(i, q_aug, jj, s_sc):
        j, _ = key_tile(i, jj)
        k0 = pl.multiple_of(j * tk, tk)
        s_sc[st_rows, :] = jnp.dot(k_ref[pl.ds(k0, tk), :], q_aug, preferred_element_type=F32)

    def accumulate(i, jj, s_sc, acc_sc, m_prev):
        j, wrapped = key_tile(i, jj)
        k0 = pl.multiple_of(j * tk, tk)
        if jj in general_visits:
            r0 = jnp.clip(j * tk - i * tq, -STRIP_LEFT, STRIP_RIGHT) + STRIP_LEFT
            bias = strip_ref[pl.ds(pl.multiple_of(r0, 128), tk), :]
            s_sc[st_rows, :] = s_sc[ld_rows, :] + jnp.concatenate([bias, bias], axis=1)
            m_new = jnp.maximum(m_prev, jnp.max(s_sc[ld_rows, :], axis=0, keepdims=True))
            shift = m_new
        else:
            c = jnp.where(wrapped, c_left, c_right)
            m_new = jnp.maximum(m_prev, jnp.max(s_sc[ld_rows, :], axis=0, keepdims=True) + c)
            shift = m_new - c
        alpha = jnp.exp2(m_prev - m_new)
        pv = None
        for c0 in range(0, tk, 256):
            rows = pl.ds(pl.multiple_of(zero_ref[0] + c0, 8), 256)
            p = jnp.exp2((s_sc[rows, :] - shift).astype(BF16))
            part = jnp.dot(vt_ref[:, pl.ds(pl.multiple_of(k0 + c0, 256), 256)], p,
                           preferred_element_type=F32)
            pv = part if pv is None else pv + part
        if jj == visit_order[0]:
            acc_sc[...] = pv
        else:
            acc_sc[...] = acc_sc[...] * alpha + pv
        return m_new

    def finish(sub, acc_sc):
        lam = (jnp.exp(jnp.sum(lq1_ref[...] * lk1_ref[...], axis=1, keepdims=True))
               - jnp.exp(jnp.sum(lq2_ref[...] * lk2_ref[...], axis=1, keepdims=True)) + lambda_init)
        o_all = acc_sc[0:E_B, :] * (1.0 / acc_sc[E_B:E_B + 1, :])
        o = o_all[:, :tq] - lam * o_all[:, tq:]
        ms = jnp.mean(o * o, axis=0, keepdims=True)
        y = o * lax.rsqrt(ms + SUBLN_EPS) * w_ref[...] * (1.0 - lambda_init)
        rows = slice(sub * tq, (sub + 1) * tq)
        o_ref[rows, :] = (y.T * _silu(g_ref[rows, :])).astype(BF16)

    nbuf = len(s_scs)
    tiles = [q_tile(sub) for sub in range(n_sub)]
    visits = [(sub, jj) for sub in range(n_sub) for jj in visit_order]

    def issue_logits(v):
        sub, jj = visits[v]
        logits(*tiles[sub], jj, s_scs[v % nbuf])

    for v in range(DIFF_AHEAD):
        issue_logits(v)
    m = None
    for v, (sub, jj) in enumerate(visits):
        if v + DIFF_AHEAD < len(visits):
            issue_logits(v + DIFF_AHEAD)
        if jj == visit_order[0]:
            m = jnp.full((1, 2 * tq), NEG_INF, F32)
        m = accumulate(tiles[sub][0], jj, s_scs[v % nbuf], acc_scs[sub], m)
        if jj == visit_order[-1]:
            finish(sub, acc_scs[sub])


def _diff_attention(qbt, kb, vbt, gb, strip, subln_w2, lq1, lk1, lq2, lk2, lambda_init):
    bsz, seq, _ = kb.shape
    tq, tk = DIFF_TQ, DIFF_TK
    nk = seq // tk
    assert nk > DIFF_AHEAD and nk >= 3
    n_sub = DIFF_SUBTILES
    assert seq % (tq * n_sub) == 0
    tq_step = tq * n_sub
    lam_spec = pl.BlockSpec((1, HEAD_DIM), lambda b, h, i: (0, 0))
    return pl.pallas_call(
        functools.partial(_diff_kernel, tq=tq, tk=tk, nk=nk, n_sub=n_sub, lambda_init=lambda_init),
        grid=(bsz, HB, seq // tq_step),
        in_specs=[pl.BlockSpec(memory_space=pltpu.SMEM),
                  pl.BlockSpec((None, E_B, tq_step), lambda b, h, i: (b, h, i)),
                  pl.BlockSpec((None, seq, E_B), lambda b, h, i: (b, 0, h)),
                  pl.BlockSpec((None, None, VB_ROWS, seq), lambda b, h, i: (b, h, 0, 0)),
                  pl.BlockSpec((None, STRIP_ROWS, tq), lambda b, h, i: (h, 0, 0)),
                  pl.BlockSpec((None, tq_step, E_B), lambda b, h, i: (b, i, h)),
                  pl.BlockSpec((E_B, 1), lambda b, h, i: (0, 0)),
                  lam_spec, lam_spec, lam_spec, lam_spec],
        out_specs=pl.BlockSpec((None, tq_step, E_B), lambda b, h, i: (b, i, h)),
        out_shape=jax.ShapeDtypeStruct((bsz, seq, D_B), BF16),
        scratch_shapes=[pltpu.VMEM((VB_ROWS, 2 * tq), F32)] * n_sub
                       + [pltpu.VMEM((tk + 8, 2 * tq), F32)] * DIFF_BUFFERS,
        compiler_params=pltpu.CompilerParams(vmem_limit_bytes=VMEM_LIMIT),
        name="diff_attention",
    )(jnp.zeros((1,), jnp.int32), qbt, kb, vbt, strip, gb, subln_w2, lq1, lk1, lq2, lk2)


def _outproj_kernel(oa_ref, ob_ref, x_ref, gate_ref, wa_ref, wb_ref, lng_ref, lnb_ref, y_ref):
    for r in range(oa_ref.shape[0] // OUT_CHUNK):
        rows = slice(r * OUT_CHUNK, (r + 1) * OUT_CHUNK)
        h = jnp.dot(oa_ref[rows, :], wa_ref[...], preferred_element_type=F32)
        h += jnp.dot(ob_ref[rows, :], wb_ref[...], preferred_element_type=F32)
        z = ALPHA * x_ref[rows, :] + gate_ref[...] * h
        mu = jnp.mean(z, axis=-1, keepdims=True)
        zc = z - mu
        var = jnp.mean(zc * zc, axis=-1, keepdims=True)
        y_ref[rows, :] = zc * lax.rsqrt(var + LN_EPS) * lng_ref[...] + lnb_ref[...]


def _out_projection(oa, ob, x, mod4, b_off, w_out_a, w_out_b, ln_g2, ln_b2):
    bsz, seq, _ = x.shape
    tm = OUT_ROW_TILE
    row = lambda w: pl.BlockSpec((None, tm, w), lambda b, t: (b, t, 0))
    full = lambda r, c: pl.BlockSpec((r, c), lambda b, t: (0, 0))
    return pl.pallas_call(
        _outproj_kernel,
        grid=(bsz, seq // tm),
        in_specs=[row(D_A), row(D_B), row(D_MODEL),
                  pl.BlockSpec((None, None, 1, D_MODEL), lambda b, t: (b + b_off, 2, 0, 0)),
                  full(D_A, D_MODEL), full(D_B, D_MODEL), full(1, D_MODEL), full(1, D_MODEL)],
        out_specs=row(D_MODEL),
        out_shape=jax.ShapeDtypeStruct((bsz, seq, D_MODEL), F32),
        compiler_params=pltpu.CompilerParams(vmem_limit_bytes=VMEM_LIMIT),
        name="out_projection",
    )(oa, ob, x, mod4, w_out_a, w_out_b, ln_g2, ln_b2)


def _bias_kernel(tab_ref, bucket_ref, o_ref, *, col0):
    h = pl.program_id(0) + col0
    bucket = bucket_ref[...]
    acc = jnp.full(bucket.shape, NEG_INF, F32)
    for b in range(N_BUCKETS):
        acc = jnp.where(bucket == b, tab_ref[b, h] * LOG2E, acc)
    o_ref[...] = acc


def _expand_bias(rel_bias, bucket, col0, n_heads, name):
    rows, cols = bucket.shape
    return pl.pallas_call(
        functools.partial(_bias_kernel, col0=col0),
        grid=(n_heads,),
        in_specs=[pl.BlockSpec(memory_space=pltpu.SMEM),
                  pl.BlockSpec((rows, cols), lambda h: (0, 0))],
        out_specs=pl.BlockSpec((None, rows, cols), lambda h: (h, 0, 0)),
        out_shape=jax.ShapeDtypeStruct((n_heads, rows, cols), F32),
        name=name,
    )(rel_bias, bucket)


def _bias_tables(rel_bias):
    kk = np.arange(3 * BLK)[:, None]
    a = np.arange(BLK)[None, :]
    rel = kk - BLK - a
    win_bucket = np.where(np.abs(rel) <= WINDOW, _t5_bucket(rel), -1).astype(np.int32)
    win = _expand_bias(rel_bias, jnp.asarray(win_bucket), 0, HA, "window_bias")
    r = np.arange(-STRIP_LEFT, STRIP_RIGHT + DIFF_TK)[:, None]
    a = np.arange(DIFF_TQ)[None, :]
    strip = _expand_bias(rel_bias, jnp.asarray(_t5_bucket(r - a)), HA, HB, "diff_bias")
    return win, strip


def _layer(x, mod4, b_off, w_std, w_fm, w_out_a, w_out_b, ln_g2, ln_b2, sink2, win_bias, strip,
           subln_w2, lq1, lk1, lq2, lk2, lambda_init):
    ka, ga, kb, gb, qat, vat, qbt, vbt = _in_projection(x, mod4, b_off, w_std, w_fm)
    oa = _window_attention(qat, ka, vat, ga, win_bias, sink2)
    ob = _diff_attention(qbt, kb, vbt, gb, strip, subln_w2, lq1, lk1, lq2, lk2, lambda_init)
    return _out_projection(oa, ob, x, mod4, b_off, w_out_a, w_out_b, ln_g2, ln_b2)


def kernel(x_prompt, x_sample, c_prompt, c_sample, w_in, w_out, w_ada, b_ada, ln_g, ln_b, attn_sink,
           lambda_q1, lambda_k1, lambda_q2, lambda_k2, subln_w, rel_bias):
    assert w_in.shape[0] == DEPTH
    nbp, nbs = c_prompt.shape[0], c_sample.shape[0]
    pad = (-(nbp + nbs)) % 16
    c_all = jnp.concatenate([c_prompt, c_sample, jnp.zeros((pad, D_MODEL), F32)], axis=0)
    win_bias, strip = _bias_tables(rel_bias)
    xp, xs = x_prompt, x_sample
    for l in range(DEPTH):
        lambda_init = 0.8 - 0.6 * math.exp(-0.3 * l)
        w = w_in[l]
        sl = lambda n: w[:, _OFF[n][0]:_OFF[n][1]]
        w_std = jnp.concatenate([sl("ka"), sl("ga"), sl("kb"), sl("gb")], axis=1).astype(BF16)
        w_fm = jnp.concatenate([sl("qa"), sl("va"), sl("qb"), sl("vb")], axis=1).T.astype(BF16)
        w_out_a = w_out[l, :D_A].astype(BF16)
        w_out_b = w_out[l, D_A:].astype(BF16)
        mod = _modulation(c_all, w_ada[l], b_ada[l])
        mod4 = mod.reshape(mod.shape[0], 3, 1, D_MODEL)
        args = (w_std, w_fm, w_out_a, w_out_b, ln_g[l].reshape(1, -1), ln_b[l].reshape(1, -1),
                attn_sink[l] * LOG2E, win_bias, strip, subln_w[l].reshape(-1, 1),
                lambda_q1[l].reshape(1, -1), lambda_k1[l].reshape(1, -1),
                lambda_q2[l].reshape(1, -1), lambda_k2[l].reshape(1, -1), lambda_init)
        xp = _layer(xp, mod4, 0, *args)
        xs = _layer(xs, mod4, nbp, *args)
    return (xp, xs)
```

```python
import functools
import math

import jax
import jax.numpy as jnp
import numpy as np
from jax import lax
from jax.experimental import pallas as pl
from jax.experimental.pallas import tpu as pltpu

F32 = jnp.float32
BF16 = jnp.bfloat16

D_MODEL = 1024
HEAD_DIM = 64
HA = 8
HKV_A = 2
G_A = HA // HKV_A
HB = 4
E_B = 2 * HEAD_DIM
ONES_ROWS = 16
VB_ROWS = E_B + ONES_ROWS
VA_ROWS = HEAD_DIM + ONES_ROWS
D_A = HA * HEAD_DIM
D_KV_A = HKV_A * HEAD_DIM
D_B = HB * E_B
WINDOW = 128
BLK = 128
N_BUCKETS = 32
HALF_BUCKETS = N_BUCKETS // 2
MAX_EXACT = HALF_BUCKETS // 2
MAX_DIST = 128
DEPTH = 1
ALPHA = (2.0 * DEPTH) ** 0.25
LN_EPS = 1e-5
SUBLN_EPS = 1e-5
NEG_INF = -1e30
LOG2E = 1.4426950408889634
Q_SCALE = HEAD_DIM ** -0.5 * LOG2E

_OFF = {}
_o = 0
for _name, _w in (("qa", D_A), ("ka", D_KV_A), ("va", D_KV_A), ("ga", D_A),
                  ("qb", D_B), ("kb", D_B), ("vb", D_B), ("gb", D_B)):
    _OFF[_name] = (_o, _o + _w)
    _o += _w

ROW_TILE = 512
OUT_ROW_TILE = 2048
OUT_CHUNK = 256
WIN_BLOCKS = 8
WIN_AHEAD = 3
DIFF_TQ = 256
DIFF_SUBTILES = 4
DIFF_TK = 512
DIFF_AHEAD = 2
DIFF_BUFFERS = DIFF_AHEAD + 2
STRIP_LEFT = DIFF_TK + MAX_DIST
STRIP_RIGHT = -(-(DIFF_TQ + MAX_DIST - 1) // 128) * 128
STRIP_ROWS = STRIP_LEFT + STRIP_RIGHT + DIFF_TK
VMEM_LIMIT = 56 * 1024 * 1024


def _t5_bucket(rel):
    assert (MAX_DIST // MAX_EXACT) ** 2 == 2 ** (HALF_BUCKETS - MAX_EXACT)
    rel = np.asarray(rel, np.int64)
    n = np.abs(rel)
    steps = sum((n * n >= MAX_EXACT ** 2 * 2 ** k).astype(np.int64)
                for k in range(1, HALF_BUCKETS - MAX_EXACT))
    large = np.minimum(MAX_EXACT + steps, HALF_BUCKETS - 1)
    return (np.where(rel > 0, HALF_BUCKETS, 0) + np.where(n < MAX_EXACT, n, large)).astype(np.int32)


def _silu(g):
    return g * jax.nn.sigmoid(g)


def _mod_kernel(c_ref, w_ref, b_ref, o_ref):
    a = _silu(c_ref[...])
    a_hi = a.astype(BF16)
    a_lo = (a - a_hi.astype(F32)).astype(BF16)
    w = w_ref[...]
    w_hi = w.astype(BF16)
    w_lo = (w - w_hi.astype(F32)).astype(BF16)
    acc = jnp.dot(a_hi, w_hi, preferred_element_type=F32)
    acc += jnp.dot(a_lo, w_hi, preferred_element_type=F32)
    acc += jnp.dot(a_hi, w_lo, preferred_element_type=F32)
    o_ref[...] = acc + b_ref[...]


def _modulation(c_all, w_ada, b_ada):
    rows = c_all.shape[0]
    n = w_ada.shape[1]
    tn = 512
    return pl.pallas_call(
        _mod_kernel,
        grid=(n // tn,),
        in_specs=[pl.BlockSpec((rows, D_MODEL), lambda j: (0, 0)),
                  pl.BlockSpec((D_MODEL, tn), lambda j: (0, j)),
                  pl.BlockSpec((1, tn), lambda j: (0, j))],
        out_specs=pl.BlockSpec((rows, tn), lambda j: (0, j)),
        out_shape=jax.ShapeDtypeStruct((rows, n), F32),
        name="modulation",
    )(c_all, w_ada, b_ada.reshape(1, n))


def _inproj_kernel(x_ref, shift_ref, scale_ref, wstd_ref, wfm_ref,
                   ka_ref, ga_ref, kb_ref, gb_ref, qat_ref, vat_ref, qbt_ref, vbt_ref):
    u = (x_ref[...] * (1.0 + scale_ref[...]) + shift_ref[...]).astype(BF16)
    r = jnp.dot(u, wstd_ref[...], preferred_element_type=F32)
    ka_ref[...] = r[:, 0:128].astype(BF16)
    ga_ref[...] = _silu(r[:, 128:640])
    kb_ref[...] = r[:, 640:1152].astype(BF16)
    gb_ref[...] = _silu(r[:, 1152:1664])
    rt = lax.dot_general(wfm_ref[...], u, (((1,), (1,)), ((), ())),
                         preferred_element_type=F32)
    qat_ref[...] = (rt[0:512] * Q_SCALE).astype(BF16)
    qbt_ref[...] = (rt[640:1152] * Q_SCALE).astype(BF16)
    ones = jnp.ones((ONES_ROWS, rt.shape[1]), BF16)
    for h in range(HKV_A):
        vat_ref[h, 0:HEAD_DIM, :] = rt[512 + h * HEAD_DIM:512 + (h + 1) * HEAD_DIM].astype(BF16)
        vat_ref[h, HEAD_DIM:VA_ROWS, :] = ones
    for h in range(HB):
        vbt_ref[h, 0:E_B, :] = rt[1152 + h * E_B:1152 + (h + 1) * E_B].astype(BF16)
        vbt_ref[h, E_B:VB_ROWS, :] = ones


def _in_projection(x, mod4, b_off, w_std, w_fm):
    bsz, seq, _ = x.shape
    tm = ROW_TILE
    n_std = w_std.shape[1]
    n_fm = w_fm.shape[0]
    row = lambda w: pl.BlockSpec((None, tm, w), lambda b, t: (b, t, 0))
    col = lambda h: pl.BlockSpec((None, h, tm), lambda b, t: (b, 0, t))
    vec = lambda k: pl.BlockSpec((None, None, 1, D_MODEL), lambda b, t: (b + b_off, k, 0, 0))
    return pl.pallas_call(
        _inproj_kernel,
        grid=(bsz, seq // tm),
        in_specs=[row(D_MODEL), vec(0), vec(1),
                  pl.BlockSpec((D_MODEL, n_std), lambda b, t: (0, 0)),
                  pl.BlockSpec((n_fm, D_MODEL), lambda b, t: (0, 0))],
        out_specs=[row(D_KV_A), row(D_A), row(D_B), row(D_B),
                   col(D_A),
                   pl.BlockSpec((None, HKV_A, VA_ROWS, tm), lambda b, t: (b, 0, 0, t)),
                   col(D_B),
                   pl.BlockSpec((None, HB, VB_ROWS, tm), lambda b, t: (b, 0, 0, t))],
        out_shape=[jax.ShapeDtypeStruct((bsz, seq, D_KV_A), BF16),
                   jax.ShapeDtypeStruct((bsz, seq, D_A), F32),
                   jax.ShapeDtypeStruct((bsz, seq, D_B), BF16),
                   jax.ShapeDtypeStruct((bsz, seq, D_B), F32),
                   jax.ShapeDtypeStruct((bsz, D_A, seq), BF16),
                   jax.ShapeDtypeStruct((bsz, HKV_A, VA_ROWS, seq), BF16),
                   jax.ShapeDtypeStruct((bsz, D_B, seq), BF16),
                   jax.ShapeDtypeStruct((bsz, HB, VB_ROWS, seq), BF16)],
        compiler_params=pltpu.CompilerParams(vmem_limit_bytes=VMEM_LIMIT),
        name="in_projection",
    )(x, mod4, mod4, w_std, w_fm)


def _win_kernel(zero_ref, sink_ref, qt_ref, kl_ref, kc_ref, kr_ref, vl_ref, vc_ref, vr_ref,
                bias_ref, g_ref, o_ref, *s_scs, n_steps, wb):
    i = pl.program_id(1)
    kall = jnp.concatenate([kl_ref[...], kc_ref[...], kr_ref[...]], axis=0)
    krow = lax.broadcasted_iota(jnp.int32, (3 * BLK, BLK), 0)
    off_l = jnp.where(krow < BLK, jnp.where(i == 0, NEG_INF, 0.0).astype(F32), 0.0)
    off_r = jnp.where(krow >= 2 * BLK, jnp.where(i == n_steps - 1, NEG_INF, 0.0).astype(F32), 0.0)
    zeros = jnp.zeros((HEAD_DIM, BLK), BF16)
    vall = [jnp.concatenate([vl_ref[hk], vc_ref[hk], vr_ref[hk]], axis=1) for hk in range(HKV_A)]
    units = [(blk, hk) for blk in range(wb) for hk in range(HKV_A)]

    def logits(u):
        blk, hk = units[u]
        blocks = []
        for g in range(G_A):
            h = hk * G_A + g
            q = qt_ref[h * HEAD_DIM:(h + 1) * HEAD_DIM, blk * BLK:(blk + 1) * BLK]
            blocks.append(jnp.concatenate([q, zeros] if hk == 0 else [zeros, q], axis=0))
        q_aug = jnp.concatenate(blocks, axis=1)
        return jnp.dot(kall[blk * BLK:(blk + 3) * BLK], q_aug, preferred_element_type=F32)

    def softmax_pv(u, s_sc):
        blk, hk = units[u]
        biases, sinks = [], []
        for g in range(G_A):
            h = hk * G_A + g
            bias = bias_ref[h]
            if blk == 0:
                bias = bias + off_l
            if blk == wb - 1:
                bias = bias + off_r
            biases.append(bias)
            sinks.append(jnp.full((1, BLK), sink_ref[h], F32))
        s_sc[rows_all, :] = s_sc[rows_all, :] + jnp.concatenate(biases, axis=1)
        sink = jnp.concatenate(sinks, axis=1)
        m = jnp.maximum(jnp.max(s_sc[rows_all, :], axis=0, keepdims=True), sink)
        pv = None
        for c0, cn in ((0, 2 * BLK), (2 * BLK, BLK)):
            rows = pl.ds(pl.multiple_of(zero_ref[0] + c0, 8), cn)
            p = jnp.exp2((s_sc[rows, :] - m).astype(BF16))
            part = jnp.dot(vall[hk][:, blk * BLK + c0:blk * BLK + c0 + cn], p,
                           preferred_element_type=F32)
            pv = part if pv is None else pv + part
        den = pv[HEAD_DIM:HEAD_DIM + 1] + jnp.exp2(sink - m)
        ot = pv[0:HEAD_DIM] * (1.0 / den)
        return [ot[:, g * BLK:(g + 1) * BLK] for g in range(G_A)]

    rows_all = pl.ds(pl.multiple_of(zero_ref[0], 8), 3 * BLK)
    nbuf = len(s_scs)
    for u in range(WIN_AHEAD):
        s_scs[u % nbuf][rows_all, :] = logits(u)
    outs = []
    for u, (blk, hk) in enumerate(units):
        if u + WIN_AHEAD < len(units):
            s_scs[(u + WIN_AHEAD) % nbuf][rows_all, :] = logits(u + WIN_AHEAD)
        outs += softmax_pv(u, s_scs[u % nbuf])
        if hk == HKV_A - 1:
            o = jnp.concatenate(outs, axis=0).T
            outs = []
            rows = slice(blk * BLK, (blk + 1) * BLK)
            o_ref[rows, :] = (o * g_ref[rows, :]).astype(BF16)


def _window_attention(qat, ka, vat, ga, bias_t, sink2):
    bsz, seq, _ = ka.shape
    nb = seq // BLK
    wb = WIN_BLOCKS
    n_steps = nb // wb
    kblk = lambda r, f: pl.BlockSpec((None, r, D_KV_A), lambda b, i: (b, f(i), 0))
    vblk = lambda r, f: pl.BlockSpec((None, HKV_A, VA_ROWS, r), lambda b, i: (b, 0, 0, f(i)))
    left = lambda i: jnp.maximum(i * wb - 1, 0)
    mid = lambda i: i
    right = lambda i: jnp.minimum((i + 1) * wb, nb - 1)
    return pl.pallas_call(
        functools.partial(_win_kernel, n_steps=n_steps, wb=wb),
        grid=(bsz, n_steps),
        in_specs=[pl.BlockSpec(memory_space=pltpu.SMEM),
                  pl.BlockSpec(memory_space=pltpu.SMEM),
                  pl.BlockSpec((None, D_A, wb * BLK), lambda b, i: (b, 0, i)),
                  kblk(BLK, left), kblk(wb * BLK, mid), kblk(BLK, right),
                  vblk(BLK, left), vblk(wb * BLK, mid), vblk(BLK, right),
                  pl.BlockSpec((HA, 3 * BLK, BLK), lambda b, i: (0, 0, 0)),
                  pl.BlockSpec((None, wb * BLK, D_A), lambda b, i: (b, i, 0))],
        out_specs=pl.BlockSpec((None, wb * BLK, D_A), lambda b, i: (b, i, 0)),
        out_shape=jax.ShapeDtypeStruct((bsz, seq, D_A), BF16),
        scratch_shapes=[pltpu.VMEM((3 * BLK + 8, G_A * BLK), F32)] * (WIN_AHEAD + 2),
        compiler_params=pltpu.CompilerParams(vmem_limit_bytes=VMEM_LIMIT),
        name="window_attention",
    )(jnp.zeros((1,), jnp.int32), sink2, qat, ka, ka, ka, vat, vat, vat, bias_t, ga)


def _diff_kernel(zero_ref, qt_ref, k_ref, vt_ref, strip_ref, g_ref, w_ref, lq1_ref, lk1_ref, lq2_ref,
                 lk2_ref, o_ref, *scratch, tq, tk, nk, n_sub, lambda_init):
    acc_scs, s_scs = scratch[:n_sub], scratch[n_sub:]
    st_rows = ld_rows = pl.ds(pl.multiple_of(zero_ref[0], 8), tk)
    c_left = jnp.concatenate([strip_ref[0:1, :]] * 2, axis=1)
    c_right = jnp.concatenate([strip_ref[STRIP_ROWS - 1:STRIP_ROWS, :]] * 2, axis=1)
    qrow = lax.broadcasted_iota(jnp.int32, (E_B, tq), 0)
    general_visits = (nk - 1, 0, 1)
    visit_order = general_visits + tuple(range(2, nk - 1))

    def q_tile(sub):
        qt = qt_ref[:, sub * tq:(sub + 1) * tq]
        zero = jnp.zeros_like(qt)
        q_aug = jnp.concatenate([jnp.where(qrow < HEAD_DIM, qt, zero),
                                 jnp.where(qrow >= HEAD_DIM, qt, zero)], axis=1)
        return pl.program_id(2) * n_sub + sub, q_aug

    def key_tile(i, jj):
        j = lax.div(i * tq, tk) + jj
        wrapped = j >= nk
        return jnp.where(wrapped, j - nk, j), wrapped

    def logits(i, q_aug, jj, s_sc):
        j, _ = key_tile(i, jj)
        k0 = pl.multiple_of(j * tk, tk)
        s_sc[st_rows, :] = jnp.dot(k_ref[pl.ds(k0, tk), :], q_aug, preferred_element_type=F32)

    def accumulate(i, jj, s_sc, acc_sc, m_prev):
        j, wrapped = key_tile(i, jj)
        k0 = pl.multiple_of(j * tk, tk)
        if jj in general_visits:
            r0 = jnp.clip(j * tk - i * tq, -STRIP_LEFT, STRIP_RIGHT) + STRIP_LEFT
            bias = strip_ref[pl.ds(pl.multiple_of(r0, 128), tk), :]
            s_sc[st_rows, :] = s_sc[ld_rows, :] + jnp.concatenate([bias, bias], axis=1)
            m_new = jnp.maximum(m_prev, jnp.max(s_sc[ld_rows, :], axis=0, keepdims=True))
            shift = m_new
        else:
            c = jnp.where(wrapped, c_left, c_right)
            m_new = jnp.maximum(m_prev, jnp.max(s_sc[ld_rows, :], axis=0, keepdims=True) + c)
            shift = m_new - c
        alpha = jnp.exp2(m_prev - m_new)
        pv = None
        for c0 in range(0, tk, 256):
            rows = pl.ds(pl.multiple_of(zero_ref[0] + c0, 8), 256)
            p = jnp.exp2((s_sc[rows, :] - shift).astype(BF16))
            part = jnp.dot(vt_ref[:, pl.ds(pl.multiple_of(k0 + c0, 256), 256)], p,
                           preferred_element_type=F32)
            pv = part if pv is None else pv + part
        if jj == visit_order[0]:
            acc_sc[...] = pv
        else:
            acc_sc[...] = acc_sc[...] * alpha + pv
        return m_new

    def finish(sub, acc_sc):
        lam = (jnp.exp(jnp.sum(lq1_ref[...] * lk1_ref[...], axis=1, keepdims=True))
               - jnp.exp(jnp.sum(lq2_ref[...] * lk2_ref[...], axis=1, keepdims=True)) + lambda_init)
        o_all = acc_sc[0:E_B, :] * (1.0 / acc_sc[E_B:E_B + 1, :])
        o = o_all[:, :tq] - lam * o_all[:, tq:]
        ms = jnp.mean(o * o, axis=0, keepdims=True)
        y = o * lax.rsqrt(ms + SUBLN_EPS) * w_ref[...] * (1.0 - lambda_init)
        rows = slice(sub * tq, (sub + 1) * tq)
        o_ref[rows, :] = (y.T * g_ref[rows, :]).astype(BF16)

    nbuf = len(s_scs)
    tiles = [q_tile(sub) for sub in range(n_sub)]
    visits = [(sub, jj) for sub in range(n_sub) for jj in visit_order]

    def issue_logits(v):
        sub, jj = visits[v]
        logits(*tiles[sub], jj, s_scs[v % nbuf])

    for v in range(DIFF_AHEAD):
        issue_logits(v)
    m = None
    for v, (sub, jj) in enumerate(visits):
        if v + DIFF_AHEAD < len(visits):
            issue_logits(v + DIFF_AHEAD)
        if jj == visit_order[0]:
            m = jnp.full((1, 2 * tq), NEG_INF, F32)
        m = accumulate(tiles[sub][0], jj, s_scs[v % nbuf], acc_scs[sub], m)
        if jj == visit_order[-1]:
            finish(sub, acc_scs[sub])


def _diff_attention(qbt, kb, vbt, gb, strip, subln_w2, lq1, lk1, lq2, lk2, lambda_init):
    bsz, seq, _ = kb.shape
    tq, tk = DIFF_TQ, DIFF_TK
    nk = seq // tk
    assert nk > DIFF_AHEAD and nk >= 3
    n_sub = DIFF_SUBTILES
    assert seq % (tq * n_sub) == 0
    tq_step = tq * n_sub
    lam_spec = pl.BlockSpec((1, HEAD_DIM), lambda b, h, i: (0, 0))
    return pl.pallas_call(
        functools.partial(_diff_kernel, tq=tq, tk=tk, nk=nk, n_sub=n_sub, lambda_init=lambda_init),
        grid=(bsz, HB, seq // tq_step),
        in_specs=[pl.BlockSpec(memory_space=pltpu.SMEM),
                  pl.BlockSpec((None, E_B, tq_step), lambda b, h, i: (b, h, i)),
                  pl.BlockSpec((None, seq, E_B), lambda b, h, i: (b, 0, h)),
                  pl.BlockSpec((None, None, VB_ROWS, seq), lambda b, h, i: (b, h, 0, 0)),
                  pl.BlockSpec((None, STRIP_ROWS, tq), lambda b, h, i: (h, 0, 0)),
                  pl.BlockSpec((None, tq_step, E_B), lambda b, h, i: (b, i, h)),
                  pl.BlockSpec((E_B, 1), lambda b, h, i: (0, 0)),
                  lam_spec, lam_spec, lam_spec, lam_spec],
        out_specs=pl.BlockSpec((None, tq_step, E_B), lambda b, h, i: (b, i, h)),
        out_shape=jax.ShapeDtypeStruct((bsz, seq, D_B), BF16),
        scratch_shapes=[pltpu.VMEM((VB_ROWS, 2 * tq), F32)] * n_sub
                       + [pltpu.VMEM((tk + 8, 2 * tq), F32)] * DIFF_BUFFERS,
        compiler_params=pltpu.CompilerParams(vmem_limit_bytes=VMEM_LIMIT),
        name="diff_attention",
    )(jnp.zeros((1,), jnp.int32), qbt, kb, vbt, strip, gb, subln_w2, lq1, lk1, lq2, lk2)


def _outproj_kernel(oa_ref, ob_ref, x_ref, gate_ref, wa_ref, wb_ref, lng_ref, lnb_ref, y_ref):
    for r in range(oa_ref.shape[0] // OUT_CHUNK):
        rows = slice(r * OUT_CHUNK, (r + 1) * OUT_CHUNK)
        h = jnp.dot(oa_ref[rows, :], wa_ref[...], preferred_element_type=F32)
        h += jnp.dot(ob_ref[rows, :], wb_ref[...], preferred_element_type=F32)
        z = ALPHA * x_ref[rows, :] + gate_ref[...] * h
        mu = jnp.mean(z, axis=-1, keepdims=True)
        zc = z - mu
        var = jnp.mean(zc * zc, axis=-1, keepdims=True)
        y_ref[rows, :] = zc * lax.rsqrt(var + LN_EPS) * lng_ref[...] + lnb_ref[...]


def _out_projection(oa, ob, x, mod4, b_off, w_out_a, w_out_b, ln_g2, ln_b2):
    bsz, seq, _ = x.shape
    tm = OUT_ROW_TILE
    row = lambda w: pl.BlockSpec((None, tm, w), lambda b, t: (b, t, 0))
    full = lambda r, c: pl.BlockSpec((r, c), lambda b, t: (0, 0))
    return pl.pallas_call(
        _outproj_kernel,
        grid=(bsz, seq // tm),
        in_specs=[row(D_A), row(D_B), row(D_MODEL),
                  pl.BlockSpec((None, None, 1, D_MODEL), lambda b, t: (b + b_off, 2, 0, 0)),
                  full(D_A, D_MODEL), full(D_B, D_MODEL), full(1, D_MODEL), full(1, D_MODEL)],
        out_specs=row(D_MODEL),
        out_shape=jax.ShapeDtypeStruct((bsz, seq, D_MODEL), F32),
        compiler_params=pltpu.CompilerParams(vmem_limit_bytes=VMEM_LIMIT),
        name="out_projection",
    )(oa, ob, x, mod4, w_out_a, w_out_b, ln_g2, ln_b2)


def _bias_kernel(tab_ref, bucket_ref, o_ref, *, col0):
    h = pl.program_id(0) + col0
    bucket = bucket_ref[...]
    acc = jnp.full(bucket.shape, NEG_INF, F32)
    for b in range(N_BUCKETS):
        acc = jnp.where(bucket == b, tab_ref[b, h] * LOG2E, acc)
    o_ref[...] = acc


def _expand_bias(rel_bias, bucket, col0, n_heads, name):
    rows, cols = bucket.shape
    return pl.pallas_call(
        functools.partial(_bias_kernel, col0=col0),
        grid=(n_heads,),
        in_specs=[pl.BlockSpec(memory_space=pltpu.SMEM),
                  pl.BlockSpec((rows, cols), lambda h: (0, 0))],
        out_specs=pl.BlockSpec((None, rows, cols), lambda h: (h, 0, 0)),
        out_shape=jax.ShapeDtypeStruct((n_heads, rows, cols), F32),
        name=name,
    )(rel_bias, bucket)


def _bias_tables(rel_bias):
    kk = np.arange(3 * BLK)[:, None]
    a = np.arange(BLK)[None, :]
    rel = kk - BLK - a
    win_bucket = np.where(np.abs(rel) <= WINDOW, _t5_bucket(rel), -1).astype(np.int32)
    win = _expand_bias(rel_bias, jnp.asarray(win_bucket), 0, HA, "window_bias")
    r = np.arange(-STRIP_LEFT, STRIP_RIGHT + DIFF_TK)[:, None]
    a = np.arange(DIFF_TQ)[None, :]
    strip = _expand_bias(rel_bias, jnp.asarray(_t5_bucket(r - a)), HA, HB, "diff_bias")
    return win, strip


def _layer(x, mod4, b_off, w_std, w_fm, w_out_a, w_out_b, ln_g2, ln_b2, sink2, win_bias, strip,
           subln_w2, lq1, lk1, lq2, lk2, lambda_init):
    ka, ga, kb, gb, qat, vat, qbt, vbt = _in_projection(x, mod4, b_off, w_std, w_fm)
    oa = _window_attention(qat, ka, vat, ga, win_bias, sink2)
    ob = _diff_attention(qbt, kb, vbt, gb, strip, subln_w2, lq1, lk1, lq2, lk2, lambda_init)
    return _out_projection(oa, ob, x, mod4, b_off, w_out_a, w_out_b, ln_g2, ln_b2)


def kernel(x_prompt, x_sample, c_prompt, c_sample, w_in, w_out, w_ada, b_ada, ln_g, ln_b, attn_sink,
           lambda_q1, lambda_k1, lambda_q2, lambda_k2, subln_w, rel_bias):
    assert w_in.shape[0] == DEPTH
    nbp, nbs = c_prompt.shape[0], c_sample.shape[0]
    pad = (-(nbp + nbs)) % 16
    c_all = jnp.concatenate([c_prompt, c_sample, jnp.zeros((pad, D_MODEL), F32)], axis=0)
    win_bias, strip = _bias_tables(rel_bias)
    xp, xs = x_prompt, x_sample
    for l in range(DEPTH):
        lambda_init = 0.8 - 0.6 * math.exp(-0.3 * l)
        w = w_in[l]
        sl = lambda n: w[:, _OFF[n][0]:_OFF[n][1]]
        w_std = jnp.concatenate([sl("ka"), sl("ga"), sl("kb"), sl("gb")], axis=1).astype(BF16)
        w_fm = jnp.concatenate([sl("qa"), sl("va"), sl("qb"), sl("vb")], axis=1).T.astype(BF16)
        w_out_a = w_out[l, :D_A].astype(BF16)
        w_out_b = w_out[l, D_A:].astype(BF16)
        mod = _modulation(c_all, w_ada[l], b_ada[l])
        mod4 = mod.reshape(mod.shape[0], 3, 1, D_MODEL)
        args = (w_std, w_fm, w_out_a, w_out_b, ln_g[l].reshape(1, -1), ln_b[l].reshape(1, -1),
                attn_sink[l] * LOG2E, win_bias, strip, subln_w[l].reshape(-1, 1),
                lambda_q1[l].reshape(1, -1), lambda_k1[l].reshape(1, -1),
                lambda_q2[l].reshape(1, -1), lambda_k2[l].reshape(1, -1), lambda_init)
        xp = _layer(xp, mod4, 0, *args)
        xs = _layer(xs, mod4, nbp, *args)
    return (xp, xs)
```
